```python
import math
import jax, jax.numpy as jnp
from jax import lax
import numpy as np

D_MODEL = 1024
BATCH = 8
SEQ = 4096
DEPTH = 1

CONV_DIM = D_MODEL
CONV_WIDTH = 31
GMLP_DIM = D_MODEL
GMLP_HEADS = 8
GMLP_HEAD_DIM = GMLP_DIM // GMLP_HEADS
CHUNK = 128
N_EXPERTS = 32
TOP_K = 4
D_EXPERT = D_MODEL
SWIGLU_ALPHA = 1.702
SWIGLU_LIMIT = 7.0
MOE_BLOCK = 256
LN_EPS = 1e-5
DEEPNORM_ALPHA = (2.0 * DEPTH) ** 0.25
DEEPNORM_BETA = (8.0 * DEPTH) ** -0.25
N_MOD = 6
IN_WIDTHS = (CONV_DIM, CONV_DIM, GMLP_DIM, GMLP_DIM, D_MODEL, D_MODEL)
IN_DIM = sum(IN_WIDTHS)
IN_SPLITS = tuple(int(s) for s in np.cumsum(IN_WIDTHS)[:-1])

kernel_name = "conformer_gmlp_moe_deepnorm_block"


def _layer_norm(x):
    xf = x.astype(jnp.float32)
    mu = jnp.mean(xf, axis=-1, keepdims=True)
    var = jnp.mean(jnp.square(xf - mu), axis=-1, keepdims=True)
    return ((xf - mu) * lax.rsqrt(var + LN_EPS)).astype(x.dtype)


def _ln_affine(x, g, b):
    return _layer_norm(x) * g + b


def _conformer_conv(a_val, a_gate, conv_w, conv_b, ln_g, ln_b, w_pa, b_pa):
    z = a_val * jax.nn.sigmoid(a_gate)
    z = jnp.pad(z, ((0, 0), (CONV_WIDTH - 1, 0), (0, 0)))
    z = lax.conv_general_dilated(
        z, conv_w[:, None, :], window_strides=(1,), padding="VALID",
        dimension_numbers=("NWC", "WIO", "NWC"),
        feature_group_count=CONV_DIM) + conv_b
    z = jax.nn.silu(_ln_affine(z, ln_g, ln_b))
    return z @ w_pa + b_pa


def _chunked_spatial_gating(u, v, ln_g, ln_b, w_s, b_s, w_pb, b_pb):
    bsz, seq, _ = u.shape
    n_chunks = seq // CHUNK
    u = jax.nn.gelu(u, approximate=False)
    v = _ln_affine(jax.nn.gelu(v, approximate=False), ln_g, ln_b)
    v = v.reshape(bsz, n_chunks, CHUNK, GMLP_HEADS, GMLP_HEAD_DIM)
    causal = jnp.tril(jnp.ones((CHUNK, CHUNK), dtype=bool))
    w_causal = jnp.where(causal[None], w_s, 0.0)
    mixed = jnp.einsum("hts,bnshd->bnthd", w_causal, v) + b_s.T[:, :, None]
    gated = u * mixed.reshape(bsz, seq, GMLP_DIM)
    return gated @ w_pb + b_pb


def _mixer(h, w_in, b_in, conv_w, conv_b, ln_a_g, ln_a_b, w_pa, b_pa,
           ln_v_g, ln_v_b, w_s, b_s, w_pb, b_pb, w_out, b_out):
    proj = h @ w_in + b_in
    a_val, a_gate, u, v, g_a, g_b = jnp.split(proj, IN_SPLITS, axis=-1)
    y_a = _conformer_conv(a_val, a_gate, conv_w, conv_b, ln_a_g, ln_a_b, w_pa, b_pa)
    y_b = _chunked_spatial_gating(u, v, ln_v_g, ln_v_b, w_s, b_s, w_pb, b_pb)
    merged = jax.nn.sigmoid(g_a) * y_a + jax.nn.sigmoid(g_b) * y_b
    return merged @ w_out + b_out


def _moe(h, w_router, b_router, w_up, b_up, w_down, b_down):
    bsz, seq, dm = h.shape
    n_tok = bsz * seq
    hf = h.reshape(n_tok, dm)
    logits = (hf @ w_router + b_router).astype(jnp.float32)
    top_logit, top_idx = lax.top_k(logits, TOP_K)
    top_w = jax.nn.softmax(top_logit, axis=-1).astype(h.dtype)
    n_assign = n_tok * TOP_K
    flat_e = top_idx.reshape(n_assign)
    flat_tok = jnp.repeat(jnp.arange(n_tok, dtype=jnp.int32), TOP_K)
    flat_w = top_w.reshape(n_assign)
    order = jnp.argsort(flat_e)
    sorted_e = flat_e[order]
    counts = jnp.bincount(flat_e, length=N_EXPERTS)
    padded = (counts + MOE_BLOCK - 1) // MOE_BLOCK * MOE_BLOCK
    start = jnp.cumsum(counts) - counts
    padded_end = jnp.cumsum(padded)
    padded_start = padded_end - padded
    dest = padded_start[sorted_e] + jnp.arange(n_assign, dtype=jnp.int32) - start[sorted_e]
    n_blocks = -(-n_assign // MOE_BLOCK) + N_EXPERTS
    n_rows = n_blocks * MOE_BLOCK
    row_tok = jnp.zeros((n_rows,), jnp.int32).at[dest].set(flat_tok[order])
    row_w = jnp.zeros((n_rows,), h.dtype).at[dest].set(flat_w[order])
    block_expert = jnp.minimum(
        jnp.searchsorted(padded_end, jnp.arange(n_blocks) * MOE_BLOCK, side="right"),
        N_EXPERTS - 1)
    xg = hf[row_tok].reshape(n_blocks, MOE_BLOCK, dm)

    def expert_block(args):
        xb, e = args
        z = xb @ w_up[e] + b_up[e]
        glu, lin = jnp.split(z, 2, axis=-1)
        glu = jnp.minimum(glu, SWIGLU_LIMIT)
        lin = jnp.clip(lin, -SWIGLU_LIMIT, SWIGLU_LIMIT)
        act = glu * jax.nn.sigmoid(SWIGLU_ALPHA * glu) * (lin + 1.0)
        return act @ w_down[e] + b_down[e]

    yg = lax.map(expert_block, (xg, block_expert)).reshape(n_rows, dm)
    out = jnp.zeros_like(hf).at[row_tok].add(yg * row_w[:, None])
    return out.reshape(bsz, seq, dm)


def setup_inputs(seed: int = 0) -> dict:
    key = jax.random.key(seed)
    ks = jax.random.split(key, 32)
    L = DEPTH

    def nrm(k, shape, scale):
        return jax.random.normal(k, shape, jnp.float32) * scale

    def gain(k, shape):
        return 1.0 + nrm(k, shape, 0.02)

    return {
        "x": nrm(ks[0], (BATCH, SEQ, D_MODEL), 1.0),
        "c": nrm(ks[1], (BATCH, D_MODEL), 1.0),
        "w_ada": nrm(ks[2], (L, D_MODEL, N_MOD * D_MODEL), 0.5 * D_MODEL ** -0.5),
        "b_ada": nrm(ks[3], (L, N_MOD * D_MODEL), 0.02),
        "w_in": nrm(ks[4], (L, D_MODEL, IN_DIM), D_MODEL ** -0.5),
        "b_in": nrm(ks[5], (L, IN_DIM), 0.02),
        "conv_w": nrm(ks[6], (L, CONV_WIDTH, CONV_DIM), CONV_WIDTH ** -0.5),
        "conv_b": nrm(ks[7], (L, CONV_DIM), 0.02),
        "ln_a_g": gain(ks[8], (L, CONV_DIM)),
        "ln_a_b": nrm(ks[9], (L, CONV_DIM), 0.02),
        "w_pa": nrm(ks[10], (L, CONV_DIM, D_MODEL), DEEPNORM_BETA * CONV_DIM ** -0.5),
        "b_pa": nrm(ks[11], (L, D_MODEL), 0.02),
        "ln_v_g": gain(ks[12], (L, GMLP_DIM)),
        "ln_v_b": nrm(ks[13], (L, GMLP_DIM), 0.02),
        "w_s": nrm(ks[14], (L, GMLP_HEADS, CHUNK, CHUNK), CHUNK ** -0.5),
        "b_s": gain(ks[15], (L, GMLP_HEADS, CHUNK)),
        "w_pb": nrm(ks[16], (L, GMLP_DIM, D_MODEL), DEEPNORM_BETA * GMLP_DIM ** -0.5),
        "b_pb": nrm(ks[17], (L, D_MODEL), 0.02),
        "w_out": nrm(ks[18], (L, D_MODEL, D_MODEL), DEEPNORM_BETA * D_MODEL ** -0.5),
        "b_out": nrm(ks[19], (L, D_MODEL), 0.02),
        "post1_g": gain(ks[20], (L, D_MODEL)),
        "post1_b": nrm(ks[21], (L, D_MODEL), 0.02),
        "w_router": nrm(ks[22], (L, D_MODEL, N_EXPERTS), D_MODEL ** -0.5),
        "b_router": nrm(ks[23], (L, N_EXPERTS), 0.01),
        "w_up": nrm(ks[24], (L, N_EXPERTS, D_MODEL, 2 * D_EXPERT), D_MODEL ** -0.5),
        "b_up": nrm(ks[25], (L, N_EXPERTS, 2 * D_EXPERT), 0.02),
        "w_down": nrm(ks[26], (L, N_EXPERTS, D_EXPERT, D_MODEL), DEEPNORM_BETA * D_EXPERT ** -0.5),
        "b_down": nrm(ks[27], (L, N_EXPERTS, D_MODEL), 0.02),
        "post2_g": gain(ks[28], (L, D_MODEL)),
        "post2_b": nrm(ks[29], (L, D_MODEL), 0.02),
    }


def reference(x, c, w_ada, b_ada, w_in, b_in, conv_w, conv_b, ln_a_g, ln_a_b, w_pa, b_pa,
              ln_v_g, ln_v_b, w_s, b_s, w_pb, b_pb, w_out, b_out, post1_g, post1_b,
              w_router, b_router, w_up, b_up, w_down, b_down, post2_g, post2_b):
    cond = jax.nn.silu(c)
    for l in range(DEPTH):
        mod = cond @ w_ada[l] + b_ada[l]
        shift1, scale1, gate1, shift2, scale2, gate2 = [
            m[:, None, :] for m in jnp.split(mod, N_MOD, axis=-1)]
        h = _layer_norm(x) * (1.0 + scale1) + shift1
        y = _mixer(h, w_in[l], b_in[l], conv_w[l], conv_b[l], ln_a_g[l], ln_a_b[l],
                   w_pa[l], b_pa[l], ln_v_g[l], ln_v_b[l], w_s[l], b_s[l],
                   w_pb[l], b_pb[l], w_out[l], b_out[l])
        x = _ln_affine(DEEPNORM_ALPHA * x + gate1 * y, post1_g[l], post1_b[l])
        h = _layer_norm(x) * (1.0 + scale2) + shift2
        y = _moe(h, w_router[l], b_router[l], w_up[l], b_up[l], w_down[l], b_down[l])
        x = _ln_affine(DEEPNORM_ALPHA * x + gate2 * y, post2_g[l], post2_b[l])
    return x
```

```python
import functools
import math

import jax
import jax.numpy as jnp
from jax import lax
from jax.experimental import pallas as pl
from jax.experimental.pallas import tpu as pltpu

D_MODEL = 1024
CONV_WIDTH = 31
GMLP_HEADS = 8
CHUNK = 128
N_EXPERTS = 32
TOP_K = 4
D_EXPERT = 1024
SWIGLU_ALPHA = 1.702
SWIGLU_LIMIT = 7.0
LN_EPS = 1e-5
DEPTH = 1
DEEPNORM_ALPHA = (2.0 * DEPTH) ** 0.25
N_MOD = 6
N_PROJ = 6

V7X_VMEM_BYTES = 64 * 1024 * 1024
V7X_SUBLANES = 8
V7X_LANES = 128

SEQ_TILE = 256
CONV_HALO = 32
CONV_ROWS = 64
CONV_COLS = 256
MOE_TILE = 2048
MOE_ROWS = 256
FINAL_TILE = 1024

F32 = jnp.float32
BF16 = jnp.bfloat16


def _ln(x):
    mu = jnp.mean(x, axis=-1, keepdims=True)
    xc = x - mu
    var = jnp.mean(xc * xc, axis=-1, keepdims=True)
    return xc * lax.rsqrt(var + LN_EPS)


def _sigmoid(x):
    return 1.0 / (1.0 + jnp.exp(-x))


def _gelu(x):
    return 0.5 * x * (1.0 + lax.erf(x * (1.0 / math.sqrt(2.0))))


def _ada_kernel(c_ref, w_ref, b_ref, o_ref):
    c = c_ref[...]
    cond = (c * _sigmoid(c)).astype(BF16)
    o_ref[...] = jnp.dot(cond, w_ref[...].astype(BF16), preferred_element_type=F32) + b_ref[...]


def _ada_call(c, w_ada, b_ada):
    bsz, d = c.shape
    n = w_ada.shape[1]
    bn = d
    return pl.pallas_call(
        _ada_kernel,
        grid=(n // bn,),
        in_specs=[pl.BlockSpec((bsz, d), lambda j: (0, 0)),
                  pl.BlockSpec((d, bn), lambda j: (0, j)),
                  pl.BlockSpec((1, bn), lambda j: (0, j))],
        out_specs=pl.BlockSpec((bsz, bn), lambda j: (0, j)),
        out_shape=jax.ShapeDtypeStruct((bsz, n), F32),
    )(c, w_ada, b_ada.reshape(1, n))


def _mixer_kernel(x_ref, mod_ref, w_in_ref, b_in_ref, cw_ref, cb_ref, lag_ref, lab_ref,
                  w_pa_ref, b_pa_ref, lvg_ref, lvb_ref, w_s_ref, bs_ref, w_pb_ref, b_pb_ref,
                  w_out_ref, b_out_ref, p1g_ref, p1b_ref, w_rt_ref, b_r_ref,
                  x1_ref, h2_ref, pos_ref, wt_ref,
                  zbuf, cbuf, mbuf, cnt):
    s = pl.program_id(1)
    ts = SEQ_TILE
    d = D_MODEL
    x = x_ref[0]
    shift1 = mod_ref[0, 0:1, :]
    scale1 = mod_ref[0, 1:2, :]
    gate1 = mod_ref[0, 2:3, :]
    shift2 = mod_ref[0, 3:4, :]
    scale2 = mod_ref[0, 4:5, :]

    h = (_ln(x) * (1.0 + scale1) + shift1).astype(BF16)

    def proj(j):
        return (jnp.dot(h, w_in_ref[:, j * d:(j + 1) * d], preferred_element_type=F32)
                + b_in_ref[:, j * d:(j + 1) * d])

    z = proj(0) * _sigmoid(proj(1))

    @pl.when(s == 0)
    def _():
        zbuf[0:CONV_HALO, :] = jnp.zeros((CONV_HALO, d), F32)

    zbuf[CONV_HALO:CONV_HALO + ts, :] = z
    first = CONV_HALO - (CONV_WIDTH - 1)
    for c0 in range(0, d, CONV_COLS):
        for r0 in range(0, ts, CONV_ROWS):
            acc = jnp.broadcast_to(cb_ref[:, c0:c0 + CONV_COLS], (CONV_ROWS, CONV_COLS))
            for k in range(CONV_WIDTH):
                acc = acc + (cw_ref[k:k + 1, c0:c0 + CONV_COLS]
                             * zbuf[first + k + r0:first + k + r0 + CONV_ROWS, c0:c0 + CONV_COLS])
            cbuf[r0:r0 + CONV_ROWS, c0:c0 + CONV_COLS] = acc
    zbuf[0:CONV_HALO, :] = zbuf[ts:ts + CONV_HALO, :]

    a = _ln(cbuf[...]) * lag_ref[...] + lab_ref[...]
    a = a * _sigmoid(a)
    y_a = jnp.dot(a.astype(BF16), w_pa_ref[...], preferred_element_type=F32) + b_pa_ref[...]

    u = _gelu(proj(2))
    v = (_ln(_gelu(proj(3))) * lvg_ref[...] + lvb_ref[...]).astype(BF16)
    row = lax.broadcasted_iota(jnp.int32, (CHUNK, CHUNK), 0)
    col = lax.broadcasted_iota(jnp.int32, (CHUNK, CHUNK), 1)
    causal = col <= row
    hd_w = d // GMLP_HEADS
    for hd in range(GMLP_HEADS):
        w_c = jnp.where(causal, w_s_ref[hd], 0.0).astype(BF16)
        for c in range(ts // CHUNK):
            blk = jnp.dot(w_c, v[c * CHUNK:(c + 1) * CHUNK, hd * hd_w:(hd + 1) * hd_w],
                          preferred_element_type=F32)
            mbuf[c * CHUNK:(c + 1) * CHUNK, hd * hd_w:(hd + 1) * hd_w] = (
                blk + bs_ref[:, hd * hd_w:(hd + 1) * hd_w])
    gated = (u * mbuf[...]).astype(BF16)
    y_b = jnp.dot(gated, w_pb_ref[...], preferred_element_type=F32) + b_pb_ref[...]

    merged = (_sigmoid(proj(4)) * y_a + _sigmoid(proj(5)) * y_b).astype(BF16)
    y = jnp.dot(merged, w_out_ref[...], preferred_element_type=F32) + b_out_ref[...]

    x1 = _ln(DEEPNORM_ALPHA * x + gate1 * y) * p1g_ref[...] + p1b_ref[...]
    x1_ref[0] = x1
    h2 = _ln(x1) * (1.0 + scale2) + shift2
    h2_ref[0] = h2.astype(BF16)

    logits = lax.dot_general(w_rt_ref[...], h2, (((1,), (1,)), ((), ())),
                             precision=lax.Precision.HIGHEST,
                             preferred_element_type=F32) + b_r_ref[...]
    e_iota = lax.broadcasted_iota(jnp.int32, (N_EXPERTS, ts), 0)
    work = logits
    sel = jnp.zeros((N_EXPERTS, ts), F32)
    top1 = None
    for k in range(TOP_K):
        m = jnp.max(work, axis=0, keepdims=True)
        idx = jnp.min(jnp.where(work == m, e_iota, N_EXPERTS), axis=0, keepdims=True)
        pick = e_iota == idx
        if k == 0:
            top1 = m
        sel = jnp.where(pick, 1.0, sel)
        work = jnp.where(pick, -jnp.inf, work)
    chosen = sel > 0.0
    ex = jnp.where(chosen, jnp.exp(logits - top1), 0.0)
    wt_ref[...] = ex / jnp.sum(ex, axis=0, keepdims=True)

    @pl.when((s * ts) % MOE_TILE == 0)
    def _():
        cnt[...] = jnp.zeros_like(cnt)

    t_row = lax.broadcasted_iota(jnp.int32, (ts, ts), 0)
    t_col = lax.broadcasted_iota(jnp.int32, (ts, ts), 1)
    before = jnp.where(t_row < t_col, 1.0, 0.0).astype(BF16)
    rank = jnp.dot(sel.astype(BF16), before, preferred_element_type=F32) + cnt[:, 0:1]
    pos_ref[...] = jnp.where(chosen, rank, -1.0)
    cnt[...] = cnt[...] + jnp.sum(sel, axis=1, keepdims=True)


def _const_spec(shape):
    nd = len(shape)
    return pl.BlockSpec(shape, lambda b, s: (0,) * nd, pipeline_mode=pl.Buffered(1))


def _mixer_call(x, mod, w_in, b_in, conv_w, conv_b, ln_a_g, ln_a_b, w_pa, b_pa, ln_v_g, ln_v_b,
                w_s, bs_full, w_pb, b_pb, w_out, b_out, post1_g, post1_b, w_rt, b_r):
    bsz, seq, d = x.shape
    ts = SEQ_TILE
    n_s = seq // ts
    n_tok = bsz * seq
    consts = [w_in, b_in, conv_w, conv_b, ln_a_g, ln_a_b, w_pa, b_pa, ln_v_g, ln_v_b,
              w_s, bs_full, w_pb, b_pb, w_out, b_out, post1_g, post1_b, w_rt, b_r]
    in_specs = [pl.BlockSpec((1, ts, d), lambda b, s: (b, s, 0)),
                pl.BlockSpec((1, N_MOD, d), lambda b, s: (b, 0, 0))]
    in_specs += [_const_spec(a.shape) for a in consts]
    out_specs = [pl.BlockSpec((1, ts, d), lambda b, s: (b, s, 0)),
                 pl.BlockSpec((1, ts, d), lambda b, s: (b, s, 0)),
                 pl.BlockSpec((N_EXPERTS, ts), lambda b, s: (0, b * n_s + s)),
                 pl.BlockSpec((N_EXPERTS, ts), lambda b, s: (0, b * n_s + s))]
    out_shape = [jax.ShapeDtypeStruct((bsz, seq, d), F32),
                 jax.ShapeDtypeStruct((bsz, seq, d), BF16),
                 jax.ShapeDtypeStruct((N_EXPERTS, n_tok), F32),
                 jax.ShapeDtypeStruct((N_EXPERTS, n_tok), F32)]
    return pl.pallas_call(
        _mixer_kernel,
        grid=(bsz, n_s),
        in_specs=in_specs,
        out_specs=out_specs,
        out_shape=out_shape,
        scratch_shapes=[pltpu.VMEM((CONV_HALO + ts, d), F32),
                        pltpu.VMEM((ts, d), F32),
                        pltpu.VMEM((ts, d), F32),
                        pltpu.VMEM((N_EXPERTS, V7X_LANES), F32)],
        compiler_params=pltpu.CompilerParams(
            dimension_semantics=("arbitrary", "arbitrary"),
            vmem_limit_bytes=V7X_VMEM_BYTES - 8 * 1024 * 1024),
    )(x, mod, *consts)


def _moe_kernel(h_ref, pos_ref, wt_ref, w_up_ref, b_up_ref, w_dn_ref, b_dn_ref, o_ref):
    e = pl.program_id(1)

    @pl.when(e == 0)
    def _():
        o_ref[...] = jnp.zeros_like(o_ref)

    pos_row = pos_ref[pl.ds(e, 1), :]
    w_row = wt_ref[pl.ds(e, 1), :]
    count = jnp.max(pos_row).astype(jnp.int32) + 1
    n_sub = (count + (MOE_ROWS - 1)) // MOE_ROWS
    f = D_EXPERT

    def body(sb, carry):
        slot = (lax.broadcasted_iota(jnp.int32, (MOE_ROWS, 1), 0) + sb * MOE_ROWS).astype(F32)
        hit = pos_row == slot
        gather = jnp.where(hit, 1.0, 0.0).astype(BF16)
        xg = jnp.dot(gather, h_ref[...], preferred_element_type=F32).astype(BF16)
        zz = jnp.dot(xg, w_up_ref[0], preferred_element_type=F32) + b_up_ref[0]
        glu = jnp.minimum(zz[:, :f], SWIGLU_LIMIT)
        lin = jnp.clip(zz[:, f:], -SWIGLU_LIMIT, SWIGLU_LIMIT)
        act = (glu * _sigmoid(SWIGLU_ALPHA * glu) * (lin + 1.0)).astype(BF16)
        y = (jnp.dot(act, w_dn_ref[0], preferred_element_type=F32) + b_dn_ref[0]).astype(BF16)
        scatter = jnp.where(hit, w_row, 0.0).astype(BF16)
        o_ref[...] += lax.dot_general(scatter, y, (((0,), (0,)), ((), ())),
                                      preferred_element_type=F32)
        return carry

    lax.fori_loop(0, n_sub, body, 0)


def _moe_call(h2, pos_t, w_t, w_up, b_up, w_down, b_down):
    n_tok, d = h2.shape
    n_tiles = n_tok // MOE_TILE
    f2 = w_up.shape[2]
    f = w_down.shape[1]
    return pl.pallas_call(
        _moe_kernel,
        grid=(n_tiles, N_EXPERTS),
        in_specs=[pl.BlockSpec((MOE_TILE, d), lambda i, e: (i, 0)),
                  pl.BlockSpec((N_EXPERTS, MOE_TILE), lambda i, e: (0, i)),
                  pl.BlockSpec((N_EXPERTS, MOE_TILE), lambda i, e: (0, i)),
                  pl.BlockSpec((1, d, f2), lambda i, e: (e, 0, 0)),
                  pl.BlockSpec((1, 1, f2), lambda i, e: (e, 0, 0)),
                  pl.BlockSpec((1, f, d), lambda i, e: (e, 0, 0)),
                  pl.BlockSpec((1, 1, d), lambda i, e: (e, 0, 0))],
        out_specs=pl.BlockSpec((MOE_TILE, d), lambda i, e: (i, 0)),
        out_shape=jax.ShapeDtypeStruct((n_tok, d), F32),
        compiler_params=pltpu.CompilerParams(
            dimension_semantics=("arbitrary", "arbitrary"),
            vmem_limit_bytes=V7X_VMEM_BYTES - 8 * 1024 * 1024),
    )(h2, pos_t, w_t, w_up, b_up, w_down, b_down)


def _final_kernel(x_ref, y_ref, mod_ref, g_ref, b_ref, o_ref):
    gate2 = mod_ref[0, 5:6, :]
    o_ref[0] = _ln(DEEPNORM_ALPHA * x_ref[0] + gate2 * y_ref[0]) * g_ref[...] + b_ref[...]


def _final_call(x1, y, mod, g, b):
    bsz, seq, d = x1.shape
    tt = FINAL_TILE
    return pl.pallas_call(
        _final_kernel,
        grid=(bsz, seq // tt),
        in_specs=[pl.BlockSpec((1, tt, d), lambda i, j: (i, j, 0)),
                  pl.BlockSpec((1, tt, d), lambda i, j: (i, j, 0)),
                  pl.BlockSpec((1, N_MOD, d), lambda i, j: (i, 0, 0)),
                  pl.BlockSpec((1, d), lambda i, j: (0, 0)),
                  pl.BlockSpec((1, d), lambda i, j: (0, 0))],
        out_specs=pl.BlockSpec((1, tt, d), lambda i, j: (i, j, 0)),
        out_shape=jax.ShapeDtypeStruct((bsz, seq, d), F32),
    )(x1, y, mod, g, b)


def kernel(x, c, w_ada, b_ada, w_in, b_in, conv_w, conv_b, ln_a_g, ln_a_b, w_pa, b_pa, ln_v_g, ln_v_b, w_s, b_s, w_pb, b_pb, w_out, b_out, post1_g, post1_b, w_router, b_router, w_up, b_up, w_down, b_down, post2_g, post2_b):
    bsz, seq, d = x.shape
    assert w_ada.shape[0] == DEPTH == 1
    assert seq % MOE_TILE == 0 and seq % SEQ_TILE == 0 and MOE_TILE % SEQ_TILE == 0
    l = 0
    row = lambda a: a.reshape(1, -1)
    mod = _ada_call(c, w_ada[l], b_ada[l]).reshape(bsz, N_MOD, d)
    bs_full = jnp.repeat(b_s[l].T, d // GMLP_HEADS, axis=1)
    x1, h2, pos_t, w_t = _mixer_call(
        x, mod, w_in[l].astype(BF16), row(b_in[l]), conv_w[l], row(conv_b[l]),
        row(ln_a_g[l]), row(ln_a_b[l]), w_pa[l].astype(BF16), row(b_pa[l]),
        row(ln_v_g[l]), row(ln_v_b[l]), w_s[l], bs_full, w_pb[l].astype(BF16), row(b_pb[l]),
        w_out[l].astype(BF16), row(b_out[l]), row(post1_g[l]), row(post1_b[l]),
        w_router[l].T, b_router[l].reshape(N_EXPERTS, 1))
    y = _moe_call(h2.reshape(bsz * seq, d), pos_t, w_t,
                  w_up[l].astype(BF16), b_up[l].reshape(N_EXPERTS, 1, -1),
                  w_down[l].astype(BF16), b_down[l].reshape(N_EXPERTS, 1, -1))
    return _final_call(x1, y.reshape(bsz, seq, d), mod, row(post2_g[l]), row(post2_b[l]))
```

```python
import functools
import math

import jax
import jax.numpy as jnp
from jax import lax
from jax.experimental import pallas as pl
from jax.experimental.pallas import tpu as pltpu

D_MODEL = 1024
CONV_WIDTH = 31
GMLP_HEADS = 8
CHUNK = 128
N_EXPERTS = 32
TOP_K = 4
D_EXPERT = 1024
SWIGLU_ALPHA = 1.702
SWIGLU_LIMIT = 7.0
LN_EPS = 1e-5
DEPTH = 1
DEEPNORM_ALPHA = (2.0 * DEPTH) ** 0.25
N_MOD = 6
N_PROJ = 6

V7X_VMEM_BYTES = 64 * 1024 * 1024
V7X_SUBLANES = 8
V7X_LANES = 128

SEQ_TILE = 256
CONV_HALO = 32
CONV_ROWS = 64
CONV_COLS = 256
MOE_TILE = 2048
MOE_GRANULE = 512
MOE_CAP = 80
MOE_GROUP = 8
MOE_ROWS = 256
FINAL_TILE = 1024
N_GRANULES = MOE_TILE // MOE_GRANULE
BF16_ROWS = 2 * V7X_SUBLANES

F32 = jnp.float32
BF16 = jnp.bfloat16


def _ln(x):
    mu = jnp.mean(x, axis=-1, keepdims=True)
    xc = x - mu
    var = jnp.mean(xc * xc, axis=-1, keepdims=True)
    return xc * lax.rsqrt(var + LN_EPS)


def _sigmoid(x):
    return 1.0 / (1.0 + jnp.exp(-x))


def _gelu(x):
    return 0.5 * x * (1.0 + lax.erf(x * (1.0 / math.sqrt(2.0))))


def _ada_kernel(c_ref, w_ref, b_ref, o_ref):
    c = c_ref[...]
    cond = (c * _sigmoid(c)).astype(BF16)
    o_ref[...] = jnp.dot(cond, w_ref[...].astype(BF16), preferred_element_type=F32) + b_ref[...]


def _ada_call(c, w_ada, b_ada):
    bsz, d = c.shape
    n = w_ada.shape[1]
    bn = d
    return pl.pallas_call(
        _ada_kernel,
        grid=(n // bn,),
        in_specs=[pl.BlockSpec((bsz, d), lambda j: (0, 0)),
                  pl.BlockSpec((d, bn), lambda j: (0, j)),
                  pl.BlockSpec((1, bn), lambda j: (0, j))],
        out_specs=pl.BlockSpec((bsz, bn), lambda j: (0, j)),
        out_shape=jax.ShapeDtypeStruct((bsz, n), F32),
    )(c, w_ada, b_ada.reshape(1, n))


def _mixer_kernel(x_ref, mod_ref, w_in_ref, b_in_ref, cw_ref, cb_ref, lag_ref, lab_ref,
                  w_pa_ref, b_pa_ref, lvg_ref, lvb_ref, w_s_ref, bs_ref, w_pb_ref, b_pb_ref,
                  w_out_ref, b_out_ref, p1g_ref, p1b_ref, w_rt_ref, b_r_ref,
                  x1_ref, h2_ref, slot_ref, wt_ref, opos_ref,
                  zbuf, zsh, cbuf, mbuf, gcnt, ocnt):
    s = pl.program_id(1)
    ts = SEQ_TILE
    d = D_MODEL
    x = x_ref[0]
    shift1 = mod_ref[0, 0:1, :]
    scale1 = mod_ref[0, 1:2, :]
    gate1 = mod_ref[0, 2:3, :]
    shift2 = mod_ref[0, 3:4, :]
    scale2 = mod_ref[0, 4:5, :]

    h = (_ln(x) * (1.0 + scale1) + shift1).astype(BF16)

    def proj(j):
        return (jnp.dot(h, w_in_ref[:, j * d:(j + 1) * d], preferred_element_type=F32)
                + b_in_ref[:, j * d:(j + 1) * d])

    z = proj(0) * _sigmoid(proj(1))

    @pl.when(s == 0)
    def _():
        zbuf[0:CONV_HALO, :] = jnp.zeros((CONV_HALO, d), F32)

    zbuf[CONV_HALO:CONV_HALO + ts, :] = z
    sh_rows = ts + CONV_HALO - V7X_SUBLANES
    for b in range(1, V7X_SUBLANES):
        zsh[b - 1] = zbuf[b:b + sh_rows, :]
    first = CONV_HALO - (CONV_WIDTH - 1)
    for c0 in range(0, d, CONV_COLS):
        for r0 in range(0, ts, CONV_ROWS):
            acc = jnp.broadcast_to(cb_ref[:, c0:c0 + CONV_COLS], (CONV_ROWS, CONV_COLS))
            for k in range(CONV_WIDTH):
                off = first + k
                a8 = off - off % V7X_SUBLANES + r0
                if off % V7X_SUBLANES == 0:
                    tap = zbuf[a8:a8 + CONV_ROWS, c0:c0 + CONV_COLS]
                else:
                    tap = zsh[off % V7X_SUBLANES - 1, a8:a8 + CONV_ROWS, c0:c0 + CONV_COLS]
                acc = acc + cw_ref[k:k + 1, c0:c0 + CONV_COLS] * tap
            cbuf[r0:r0 + CONV_ROWS, c0:c0 + CONV_COLS] = acc
    zbuf[0:CONV_HALO, :] = zbuf[ts:ts + CONV_HALO, :]

    a = _ln(cbuf[...]) * lag_ref[...] + lab_ref[...]
    a = a * _sigmoid(a)
    y_a = jnp.dot(a.astype(BF16), w_pa_ref[...], preferred_element_type=F32) + b_pa_ref[...]

    u = _gelu(proj(2))
    v = (_ln(_gelu(proj(3))) * lvg_ref[...] + lvb_ref[...]).astype(BF16)
    row = lax.broadcasted_iota(jnp.int32, (CHUNK, CHUNK), 0)
    col = lax.broadcasted_iota(jnp.int32, (CHUNK, CHUNK), 1)
    causal = col <= row
    hd_w = d // GMLP_HEADS
    for hd in range(GMLP_HEADS):
        w_c = jnp.where(causal, w_s_ref[hd], 0.0).astype(BF16)
        for c in range(ts // CHUNK):
            blk = jnp.dot(w_c, v[c * CHUNK:(c + 1) * CHUNK, hd * hd_w:(hd + 1) * hd_w],
                          preferred_element_type=F32)
            mbuf[c * CHUNK:(c + 1) * CHUNK, hd * hd_w:(hd + 1) * hd_w] = (
                blk + bs_ref[:, hd * hd_w:(hd + 1) * hd_w])
    gated = (u * mbuf[...]).astype(BF16)
    y_b = jnp.dot(gated, w_pb_ref[...], preferred_element_type=F32) + b_pb_ref[...]

    merged = (_sigmoid(proj(4)) * y_a + _sigmoid(proj(5)) * y_b).astype(BF16)
    y = jnp.dot(merged, w_out_ref[...], preferred_element_type=F32) + b_out_ref[...]

    x1 = _ln(DEEPNORM_ALPHA * x + gate1 * y) * p1g_ref[...] + p1b_ref[...]
    x1_ref[0] = x1
    h2 = _ln(x1) * (1.0 + scale2) + shift2
    h2_ref[0] = h2.astype(BF16)

    logits = lax.dot_general(w_rt_ref[...], h2, (((1,), (1,)), ((), ())),
                             precision=lax.Precision.HIGHEST,
                             preferred_element_type=F32) + b_r_ref[...]
    e_iota = lax.broadcasted_iota(jnp.int32, (N_EXPERTS, ts), 0)
    work = logits
    sel = jnp.zeros((N_EXPERTS, ts), F32)
    top1 = None
    for k in range(TOP_K):
        m = jnp.max(work, axis=0, keepdims=True)
        idx = jnp.min(jnp.where(work == m, e_iota, N_EXPERTS), axis=0, keepdims=True)
        pick = e_iota == idx
        if k == 0:
            top1 = m
        sel = jnp.where(pick, 1.0, sel)
        work = jnp.where(pick, -jnp.inf, work)
    chosen = sel > 0.0
    ex = jnp.where(chosen, jnp.exp(logits - top1), 0.0)
    wt_ref[...] = ex / jnp.sum(ex, axis=0, keepdims=True)

    @pl.when((s * ts) % MOE_GRANULE == 0)
    def _():
        gcnt[...] = jnp.zeros_like(gcnt)

    @pl.when((s * ts) % MOE_TILE == 0)
    def _():
        ocnt[...] = jnp.zeros_like(ocnt)

    t_row = lax.broadcasted_iota(jnp.int32, (ts, ts), 0)
    t_col = lax.broadcasted_iota(jnp.int32, (ts, ts), 1)
    before = jnp.where(t_row < t_col, 1.0, 0.0).astype(BF16)
    rank = jnp.dot(sel.astype(BF16), before, preferred_element_type=F32) + gcnt[:, 0:1]
    fits = jnp.logical_and(chosen, rank < float(MOE_CAP))
    slot_ref[...] = jnp.where(fits, rank, -1.0)
    over = jnp.where(jnp.logical_and(chosen, rank >= float(MOE_CAP)), 1.0, 0.0)
    orank = jnp.dot(over.astype(BF16), before, preferred_element_type=F32) + ocnt[:, 0:1]
    opos_ref[...] = jnp.where(over > 0.0, orank, -1.0)
    gcnt[...] = gcnt[...] + jnp.sum(sel, axis=1, keepdims=True)
    ocnt[...] = ocnt[...] + jnp.sum(over, axis=1, keepdims=True)


def _const_spec(shape):
    nd = len(shape)
    return pl.BlockSpec(shape, lambda b, s: (0,) * nd, pipeline_mode=pl.Buffered(1))


def _mixer_call(x, mod, w_in, b_in, conv_w, conv_b, ln_a_g, ln_a_b, w_pa, b_pa, ln_v_g, ln_v_b,
                w_s, bs_full, w_pb, b_pb, w_out, b_out, post1_g, post1_b, w_rt, b_r):
    bsz, seq, d = x.shape
    ts = SEQ_TILE
    n_s = seq // ts
    n_tok = bsz * seq
    consts = [w_in, b_in, conv_w, conv_b, ln_a_g, ln_a_b, w_pa, b_pa, ln_v_g, ln_v_b,
              w_s, bs_full, w_pb, b_pb, w_out, b_out, post1_g, post1_b, w_rt, b_r]
    in_specs = [pl.BlockSpec((1, ts, d), lambda b, s: (b, s, 0)),
                pl.BlockSpec((1, N_MOD, d), lambda b, s: (b, 0, 0))]
    in_specs += [_const_spec(a.shape) for a in consts]
    route_spec = pl.BlockSpec((N_EXPERTS, ts), lambda b, s: (0, b * n_s + s))
    route_shape = jax.ShapeDtypeStruct((N_EXPERTS, n_tok), F32)
    out_specs = [pl.BlockSpec((1, ts, d), lambda b, s: (b, s, 0)),
                 pl.BlockSpec((1, ts, d), lambda b, s: (b, s, 0)),
                 route_spec, route_spec, route_spec]
    out_shape = [jax.ShapeDtypeStruct((bsz, seq, d), F32),
                 jax.ShapeDtypeStruct((bsz, seq, d), BF16),
                 route_shape, route_shape, route_shape]
    return pl.pallas_call(
        _mixer_kernel,
        grid=(bsz, n_s),
        in_specs=in_specs,
        out_specs=out_specs,
        out_shape=out_shape,
        scratch_shapes=[pltpu.VMEM((CONV_HALO + ts, d), F32),
                        pltpu.VMEM((V7X_SUBLANES - 1, ts + CONV_HALO - V7X_SUBLANES, d), F32),
                        pltpu.VMEM((ts, d), F32),
                        pltpu.VMEM((ts, d), F32),
                        pltpu.VMEM((N_EXPERTS, V7X_LANES), F32),
                        pltpu.VMEM((N_EXPERTS, V7X_LANES), F32)],
        compiler_params=pltpu.CompilerParams(
            dimension_semantics=("arbitrary", "arbitrary"),
            vmem_limit_bytes=V7X_VMEM_BYTES - 8 * 1024 * 1024),
    )(x, mod, *consts)


def _expert(xg, w_up_ref, b_up_ref, w_dn_ref, b_dn_ref):
    f = D_EXPERT
    zz = jnp.dot(xg, w_up_ref[0], preferred_element_type=F32) + b_up_ref[0]
    glu = jnp.minimum(zz[:, :f], SWIGLU_LIMIT)
    lin = jnp.clip(zz[:, f:], -SWIGLU_LIMIT, SWIGLU_LIMIT)
    act = (glu * _sigmoid(SWIGLU_ALPHA * glu) * (lin + 1.0)).astype(BF16)
    return (jnp.dot(act, w_dn_ref[0], preferred_element_type=F32) + b_dn_ref[0]).astype(BF16)


def _moe_kernel(h_ref, slot_ref, wt_ref, opos_ref, w_up_ref, b_up_ref, w_dn_ref, b_dn_ref,
                o_ref, xs, ys):
    e = pl.program_id(1)
    member = e % MOE_GROUP
    base = pl.multiple_of(e - member, MOE_GROUP)
    gr = MOE_GRANULE

    @pl.when(e == 0)
    def _():
        o_ref[...] = jnp.zeros_like(o_ref)

    def onehot(g, weighted):
        slots = slot_ref[pl.ds(base, MOE_GROUP), g * gr:(g + 1) * gr]
        if weighted:
            wts = wt_ref[pl.ds(base, MOE_GROUP), g * gr:(g + 1) * gr]
        cap = lax.broadcasted_iota(jnp.int32, (MOE_CAP, 1), 0).astype(F32)
        blocks = []
        for m in range(MOE_GROUP):
            hit = slots[m:m + 1, :] == cap
            val = wts[m:m + 1, :] if weighted else 1.0
            blocks.append(jnp.where(hit, val, 0.0).astype(BF16))
        return jnp.concatenate(blocks, axis=0)

    @pl.when(member == 0)
    def _():
        for g in range(N_GRANULES):
            xs[g] = jnp.dot(onehot(g, False), h_ref[g * gr:(g + 1) * gr, :],
                            preferred_element_type=F32).astype(BF16)

    r0 = pl.multiple_of(member * MOE_CAP, BF16_ROWS)
    xe = jnp.concatenate([xs[g, pl.ds(r0, MOE_CAP), :] for g in range(N_GRANULES)], axis=0)
    ye = _expert(xe, w_up_ref, b_up_ref, w_dn_ref, b_dn_ref)
    for g in range(N_GRANULES):
        ys[g, pl.ds(r0, MOE_CAP), :] = ye[g * MOE_CAP:(g + 1) * MOE_CAP, :]

    @pl.when(member == MOE_GROUP - 1)
    def _():
        for g in range(N_GRANULES):
            o_ref[g * gr:(g + 1) * gr, :] += lax.dot_general(
                onehot(g, True), ys[g], (((0,), (0,)), ((), ())), preferred_element_type=F32)

    pos_row = opos_ref[pl.ds(e, 1), :]
    w_row = wt_ref[pl.ds(e, 1), :]
    count = jnp.max(pos_row).astype(jnp.int32) + 1
    n_sub = (count + (MOE_ROWS - 1)) // MOE_ROWS

    def body(sb, carry):
        slot = (lax.broadcasted_iota(jnp.int32, (MOE_ROWS, 1), 0) + sb * MOE_ROWS).astype(F32)
        hit = pos_row == slot
        gather = jnp.where(hit, 1.0, 0.0).astype(BF16)
        xg = jnp.dot(gather, h_ref[...], preferred_element_type=F32).astype(BF16)
        y = _expert(xg, w_up_ref, b_up_ref, w_dn_ref, b_dn_ref)
        scatter = jnp.where(hit, w_row, 0.0).astype(BF16)
        o_ref[...] += lax.dot_general(scatter, y, (((0,), (0,)), ((), ())),
                                      preferred_element_type=F32)
        return carry

    lax.fori_loop(0, n_sub, body, 0)


def _moe_call(h2, slot_t, w_t, opos_t, w_up, b_up, w_down, b_down):
    n_tok, d = h2.shape
    n_tiles = n_tok // MOE_TILE
    f2 = w_up.shape[2]
    f = w_down.shape[1]
    route_spec = pl.BlockSpec((N_EXPERTS, MOE_TILE), lambda i, e: (0, i))
    return pl.pallas_call(
        _moe_kernel,
        grid=(n_tiles, N_EXPERTS),
        in_specs=[pl.BlockSpec((MOE_TILE, d), lambda i, e: (i, 0), pipeline_mode=pl.Buffered(1)),
                  route_spec, route_spec, route_spec,
                  pl.BlockSpec((1, d, f2), lambda i, e: (e, 0, 0)),
                  pl.BlockSpec((1, 1, f2), lambda i, e: (e, 0, 0)),
                  pl.BlockSpec((1, f, d), lambda i, e: (e, 0, 0)),
                  pl.BlockSpec((1, 1, d), lambda i, e: (e, 0, 0))],
        out_specs=pl.BlockSpec((MOE_TILE, d), lambda i, e: (i, 0)),
        out_shape=jax.ShapeDtypeStruct((n_tok, d), F32),
        scratch_shapes=[pltpu.VMEM((N_GRANULES, MOE_GROUP * MOE_CAP, d), BF16),
                        pltpu.VMEM((N_GRANULES, MOE_GROUP * MOE_CAP, d), BF16)],
        compiler_params=pltpu.CompilerParams(
            dimension_semantics=("arbitrary", "arbitrary"),
            vmem_limit_bytes=V7X_VMEM_BYTES - 4 * 1024 * 1024),
    )(h2, slot_t, w_t, opos_t, w_up, b_up, w_down, b_down)


def _final_kernel(x_ref, y_ref, mod_ref, g_ref, b_ref, o_ref):
    gate2 = mod_ref[0, 5:6, :]
    o_ref[0] = _ln(DEEPNORM_ALPHA * x_ref[0] + gate2 * y_ref[0]) * g_ref[...] + b_ref[...]


def _final_call(x1, y, mod, g, b):
    bsz, seq, d = x1.shape
    tt = FINAL_TILE
    return pl.pallas_call(
        _final_kernel,
        grid=(bsz, seq // tt),
        in_specs=[pl.BlockSpec((1, tt, d), lambda i, j: (i, j, 0)),
                  pl.BlockSpec((1, tt, d), lambda i, j: (i, j, 0)),
                  pl.BlockSpec((1, N_MOD, d), lambda i, j: (i, 0, 0)),
                  pl.BlockSpec((1, d), lambda i, j: (0, 0)),
                  pl.BlockSpec((1, d), lambda i, j: (0, 0))],
        out_specs=pl.BlockSpec((1, tt, d), lambda i, j: (i, j, 0)),
        out_shape=jax.ShapeDtypeStruct((bsz, seq, d), F32),
    )(x1, y, mod, g, b)


def kernel(x, c, w_ada, b_ada, w_in, b_in, conv_w, conv_b, ln_a_g, ln_a_b, w_pa, b_pa, ln_v_g, ln_v_b, w_s, b_s, w_pb, b_pb, w_out, b_out, post1_g, post1_b, w_router, b_router, w_up, b_up, w_down, b_down, post2_g, post2_b):
    bsz, seq, d = x.shape
    assert w_ada.shape[0] == DEPTH == 1
    assert seq % MOE_TILE == 0 and seq % SEQ_TILE == 0 and MOE_GRANULE % SEQ_TILE == 0
    assert MOE_TILE % MOE_GRANULE == 0 and MOE_CAP % BF16_ROWS == 0
    assert N_EXPERTS % MOE_GROUP == 0 and MOE_GROUP == V7X_SUBLANES
    l = 0
    row = lambda a: a.reshape(1, -1)
    mod = _ada_call(c, w_ada[l], b_ada[l]).reshape(bsz, N_MOD, d)
    bs_full = jnp.repeat(b_s[l].T, d // GMLP_HEADS, axis=1)
    x1, h2, slot_t, w_t, opos_t = _mixer_call(
        x, mod, w_in[l].astype(BF16), row(b_in[l]), conv_w[l], row(conv_b[l]),
        row(ln_a_g[l]), row(ln_a_b[l]), w_pa[l].astype(BF16), row(b_pa[l]),
        row(ln_v_g[l]), row(ln_v_b[l]), w_s[l], bs_full, w_pb[l].astype(BF16), row(b_pb[l]),
        w_out[l].astype(BF16), row(b_out[l]), row(post1_g[l]), row(post1_b[l]),
        w_router[l].T, b_router[l].reshape(N_EXPERTS, 1))
    y = _moe_call(h2.reshape(bsz * seq, d), slot_t, w_t, opos_t,
                  w_up[l].astype(BF16), b_up[l].reshape(N_EXPERTS, 1, -1),
                  w_down[l].astype(BF16), b_down[l].reshape(N_EXPERTS, 1, -1))
    return _final_call(x1, y.reshape(bsz, seq, d), mod, row(post2_g[l]), row(post2_b[l]))
```

```python
import math

import jax
import jax.numpy as jnp
from jax import lax
from jax.experimental import pallas as pl
from jax.experimental.pallas import tpu as pltpu

D_MODEL = 1024
CONV_WIDTH = 31
GMLP_HEADS = 8
CHUNK = 128
N_EXPERTS = 32
TOP_K = 4
D_EXPERT = 1024
SWIGLU_ALPHA = 1.702
SWIGLU_LIMIT = 7.0
LN_EPS = 1e-5
DEPTH = 1
DEEPNORM_ALPHA = (2.0 * DEPTH) ** 0.25
N_MOD = 6
N_PROJ = 6

V7X_VMEM_BYTES = 64 * 1024 * 1024
V7X_SUBLANES = 8
V7X_LANES = 128
BF16_ROWS = 2 * V7X_SUBLANES

SEQ_TILE = 512
SUB_TILE = 256
CONV_HALO = 32
CONV_ROWS = 64
CONV_COLS = 256
MOE_TILE = 2048
MOE_GRANULE = SEQ_TILE
MOE_CAP = 80
MOE_GROUP = 8
MOE_ROWS = 256
UP_SPLIT = 4
DOWN_SPLIT = 4
FINAL_TILE = 1024
N_GRANULES = MOE_TILE // MOE_GRANULE

F32 = jnp.float32
BF16 = jnp.bfloat16


def _ln(x):
    mu = jnp.mean(x, axis=-1, keepdims=True)
    xc = x - mu
    var = jnp.mean(xc * xc, axis=-1, keepdims=True)
    return xc * lax.rsqrt(var + LN_EPS)


def _sigmoid(x):
    return 1.0 / (1.0 + jnp.exp(-x))


def _gelu(x):
    return 0.5 * x * (1.0 + lax.erf(x * (1.0 / math.sqrt(2.0))))


def _ada_kernel(c_ref, w_ref, b_ref, o_ref):
    c = c_ref[...]
    cond = (c * _sigmoid(c)).astype(BF16)
    o_ref[...] = jnp.dot(cond, w_ref[...].astype(BF16), preferred_element_type=F32) + b_ref[...]


def _ada_call(c, w_ada, b_ada):
    bsz, d = c.shape
    n = w_ada.shape[1]
    bn = d
    return pl.pallas_call(
        _ada_kernel,
        grid=(n // bn,),
        in_specs=[pl.BlockSpec((bsz, d), lambda j: (0, 0)),
                  pl.BlockSpec((d, bn), lambda j: (0, j)),
                  pl.BlockSpec((1, bn), lambda j: (0, j))],
        out_specs=pl.BlockSpec((bsz, bn), lambda j: (0, j)),
        out_shape=jax.ShapeDtypeStruct((bsz, n), F32),
    )(c, w_ada, b_ada.reshape(1, n))


def _mixer_kernel(x_ref, mod_ref, w_in_ref, b_in_ref, cw_ref, cb_ref, lag_ref, lab_ref,
                  w_pa_ref, b_pa_ref, lvg_ref, lvb_ref, w_s_ref, bs_ref, w_pb_ref, b_pb_ref,
                  w_out_ref, b_out_ref, p1g_ref, p1b_ref, w_rt_ref, b_r_ref,
                  x1_ref, h2_ref, slot_ref, wt_ref, opos_ref,
                  zbuf, zsh, cbuf, mbuf, pbuf, ocnt):
    s = pl.program_id(1)
    ts = SEQ_TILE
    st = SUB_TILE
    d = D_MODEL

    @pl.when(s == 0)
    def _():
        zbuf[0:CONV_HALO, :] = jnp.zeros((CONV_HALO, d), F32)

    @pl.when((s * ts) % MOE_TILE == 0)
    def _():
        ocnt[...] = jnp.zeros_like(ocnt)

    shift1 = mod_ref[0, 0:1, :]
    scale1 = mod_ref[0, 1:2, :]
    gate1 = mod_ref[0, 2:3, :]
    shift2 = mod_ref[0, 3:4, :]
    scale2 = mod_ref[0, 4:5, :]

    row = lax.broadcasted_iota(jnp.int32, (CHUNK, CHUNK), 0)
    col = lax.broadcasted_iota(jnp.int32, (CHUNK, CHUNK), 1)
    causal = col <= row
    w_mix = [jnp.where(causal, w_s_ref[hd], 0.0).astype(BF16) for hd in range(GMLP_HEADS)]
    t_row = lax.broadcasted_iota(jnp.int32, (st, st), 0)
    t_col = lax.broadcasted_iota(jnp.int32, (st, st), 1)
    before = jnp.where(t_row < t_col, 1.0, 0.0).astype(BF16)
    e_iota = lax.broadcasted_iota(jnp.int32, (N_EXPERTS, st), 0)
    first = CONV_HALO - (CONV_WIDTH - 1)
    sh_rows = st + CONV_HALO - V7X_SUBLANES
    hd_w = d // GMLP_HEADS

    n_chains = ts // st
    n_slabs = d // CONV_COLS
    xs_in = [x_ref[0, ci * st:(ci + 1) * st, :] for ci in range(n_chains)]
    hs = [(_ln(xv) * (1.0 + scale1) + shift1).astype(BF16) for xv in xs_in]

    def proj_slab(ci, j, n):
        lo = j * d + n * CONV_COLS
        return (jnp.dot(hs[ci], w_in_ref[:, lo:lo + CONV_COLS], preferred_element_type=F32)
                + b_in_ref[:, lo:lo + CONV_COLS])

    def glu_slab(ci, n):
        zbuf[CONV_HALO + ci * st:CONV_HALO + (ci + 1) * st, n * CONV_COLS:(n + 1) * CONV_COLS] = (
            proj_slab(ci, 0, n) * _sigmoid(proj_slab(ci, 1, n)))

    def late_slab(ci, j, n):
        pbuf[ci, j - 2, :, n * CONV_COLS:(n + 1) * CONV_COLS] = proj_slab(ci, j, n)

    def conv_block(ci, c0, r0):
        r = ci * st
        if r0 == 0:
            for b in range(1, V7X_SUBLANES):
                zsh[ci, b - 1] = zbuf[r + b:r + b + sh_rows, c0:c0 + CONV_COLS]
        acc = jnp.broadcast_to(cb_ref[:, c0:c0 + CONV_COLS], (CONV_ROWS, CONV_COLS))
        for k in range(CONV_WIDTH):
            off = first + k
            b = off % V7X_SUBLANES
            a8 = off - b + r0
            if b == 0:
                tap = zbuf[r + a8:r + a8 + CONV_ROWS, c0:c0 + CONV_COLS]
            else:
                tap = zsh[ci, b - 1, a8:a8 + CONV_ROWS, :]
            acc = acc + cw_ref[k:k + 1, c0:c0 + CONV_COLS] * tap
        cbuf[ci, r0:r0 + CONV_ROWS, c0:c0 + CONV_COLS] = acc

    for n in range(n_slabs):
        glu_slab(0, n)
    mxu_items = [(glu_slab, (ci, n)) for ci in range(1, n_chains) for n in range(n_slabs)]
    mxu_items += [(late_slab, (ci, j, n)) for ci in range(n_chains)
                  for j in range(2, N_PROJ) for n in range(n_slabs)]
    blocks = [(ci, c0, r0) for ci in range(n_chains) for c0 in range(0, d, CONV_COLS)
              for r0 in range(0, st, CONV_ROWS)]
    extra = len(mxu_items) - len(blocks)
    assert 0 <= extra <= len(blocks)
    for bi, blk in enumerate(blocks):
        conv_block(*blk)
        for _ in range(2 if bi < extra else 1):
            fn, args = mxu_items.pop(0)
            fn(*args)
    assert not mxu_items

    def chain(ci, gbase, obase):
        r = ci * st
        x = xs_in[ci]
        a = _ln(cbuf[ci]) * lag_ref[...] + lab_ref[...]
        a = a * _sigmoid(a)
        y_a = jnp.dot(a.astype(BF16), w_pa_ref[...], preferred_element_type=F32) + b_pa_ref[...]

        u = _gelu(pbuf[ci, 0])
        v = (_ln(_gelu(pbuf[ci, 1])) * lvg_ref[...] + lvb_ref[...]).astype(BF16)
        for hd in range(GMLP_HEADS):
            for c in range(st // CHUNK):
                blk = jnp.dot(w_mix[hd], v[c * CHUNK:(c + 1) * CHUNK, hd * hd_w:(hd + 1) * hd_w],
                              preferred_element_type=F32)
                mbuf[ci, c * CHUNK:(c + 1) * CHUNK, hd * hd_w:(hd + 1) * hd_w] = (
                    blk + bs_ref[:, hd * hd_w:(hd + 1) * hd_w])
        gated = (u * mbuf[ci]).astype(BF16)
        y_b = jnp.dot(gated, w_pb_ref[...], preferred_element_type=F32) + b_pb_ref[...]

        merged = (_sigmoid(pbuf[ci, 2]) * y_a + _sigmoid(pbuf[ci, 3]) * y_b).astype(BF16)
        y = jnp.dot(merged, w_out_ref[...], preferred_element_type=F32) + b_out_ref[...]

        x1 = _ln(DEEPNORM_ALPHA * x + gate1 * y) * p1g_ref[...] + p1b_ref[...]
        x1_ref[0, r:r + st, :] = x1
        h2 = _ln(x1) * (1.0 + scale2) + shift2
        h2_ref[0, r:r + st, :] = h2.astype(BF16)

        logits = lax.dot_general(w_rt_ref[...], h2, (((1,), (1,)), ((), ())),
                                 precision=lax.Precision.HIGHEST,
                                 preferred_element_type=F32) + b_r_ref[...]
        work = logits
        sel = jnp.zeros((N_EXPERTS, st), F32)
        top1 = None
        for k in range(TOP_K):
            m = jnp.max(work, axis=0, keepdims=True)
            idx = jnp.min(jnp.where(work == m, e_iota, N_EXPERTS), axis=0, keepdims=True)
            pick = e_iota == idx
            if k == 0:
                top1 = m
            sel = jnp.where(pick, 1.0, sel)
            work = jnp.where(pick, -jnp.inf, work)
        chosen = sel > 0.0
        ex = jnp.where(chosen, jnp.exp(logits - top1), 0.0)
        wt_ref[:, r:r + st] = ex / jnp.sum(ex, axis=0, keepdims=True)

        rank = jnp.dot(sel.astype(BF16), before, preferred_element_type=F32) + gbase
        fits = jnp.logical_and(chosen, rank < float(MOE_CAP))
        slot_ref[:, r:r + st] = jnp.where(fits, rank, -1.0)
        over = jnp.where(jnp.logical_and(chosen, rank >= float(MOE_CAP)), 1.0, 0.0)
        orank = jnp.dot(over.astype(BF16), before, preferred_element_type=F32) + obase
        opos_ref[:, r:r + st] = jnp.where(over > 0.0, orank, -1.0)
        return (gbase + jnp.sum(sel, axis=1, keepdims=True),
                obase + jnp.sum(over, axis=1, keepdims=True))

    gbase = jnp.zeros((N_EXPERTS, 1), F32)
    obase = ocnt[:, 0:1]
    for ci in range(ts // st):
        gbase, obase = chain(ci, gbase, obase)
    ocnt[...] = jnp.broadcast_to(obase, ocnt.shape)
    zbuf[0:CONV_HALO, :] = zbuf[ts:ts + CONV_HALO, :]


def _const_spec(shape):
    nd = len(shape)
    return pl.BlockSpec(shape, lambda b, s: (0,) * nd, pipeline_mode=pl.Buffered(1))


def _mixer_call(x, mod, w_in, b_in, conv_w, conv_b, ln_a_g, ln_a_b, w_pa, b_pa, ln_v_g, ln_v_b,
                w_s, bs_full, w_pb, b_pb, w_out, b_out, post1_g, post1_b, w_rt, b_r):
    bsz, seq, d = x.shape
    ts = SEQ_TILE
    n_chains = ts // SUB_TILE
    n_s = seq // ts
    n_tok = bsz * seq
    consts = [w_in, b_in, conv_w, conv_b, ln_a_g, ln_a_b, w_pa, b_pa, ln_v_g, ln_v_b,
              w_s, bs_full, w_pb, b_pb, w_out, b_out, post1_g, post1_b, w_rt, b_r]
    in_specs = [pl.BlockSpec((1, ts, d), lambda b, s: (b, s, 0)),
                pl.BlockSpec((1, N_MOD, d), lambda b, s: (b, 0, 0))]
    in_specs += [_const_spec(a.shape) for a in consts]
    route_spec = pl.BlockSpec((N_EXPERTS, ts), lambda b, s: (0, b * n_s + s))
    route_shape = jax.ShapeDtypeStruct((N_EXPERTS, n_tok), F32)
    out_specs = [pl.BlockSpec((1, ts, d), lambda b, s: (b, s, 0)),
                 pl.BlockSpec((1, ts, d), lambda b, s: (b, s, 0)),
                 route_spec, route_spec, route_spec]
    out_shape = [jax.ShapeDtypeStruct((bsz, seq, d), F32),
                 jax.ShapeDtypeStruct((bsz, seq, d), BF16),
                 route_shape, route_shape, route_shape]
    sh_rows = SUB_TILE + CONV_HALO - V7X_SUBLANES
    return pl.pallas_call(
        _mixer_kernel,
        grid=(bsz, n_s),
        in_specs=in_specs,
        out_specs=out_specs,
        out_shape=out_shape,
        scratch_shapes=[pltpu.VMEM((CONV_HALO + ts, d), F32),
                        pltpu.VMEM((n_chains, V7X_SUBLANES - 1, sh_rows, CONV_COLS), F32),
                        pltpu.VMEM((n_chains, SUB_TILE, d), F32),
                        pltpu.VMEM((n_chains, SUB_TILE, d), F32),
                        pltpu.VMEM((n_chains, N_PROJ - 2, SUB_TILE, d), F32),
                        pltpu.VMEM((N_EXPERTS, V7X_LANES), F32)],
        compiler_params=pltpu.CompilerParams(
            dimension_semantics=("arbitrary", "arbitrary"),
            vmem_limit_bytes=V7X_VMEM_BYTES - 8 * 1024 * 1024),
    )(x, mod, *consts)


def _expert(xg, w_up_refs, b_up_ref, w_dn_refs, b_dn_ref):
    f = D_EXPERT
    zz = jnp.concatenate([jnp.dot(xg, w[0, 0], preferred_element_type=F32) for w in w_up_refs],
                         axis=1) + b_up_ref[0]
    glu = jnp.minimum(zz[:, :f], SWIGLU_LIMIT)
    lin = jnp.clip(zz[:, f:], -SWIGLU_LIMIT, SWIGLU_LIMIT)
    act = (glu * _sigmoid(SWIGLU_ALPHA * glu) * (lin + 1.0)).astype(BF16)
    y = jnp.concatenate([jnp.dot(act, w[0, 0], preferred_element_type=F32) for w in w_dn_refs],
                        axis=1) + b_dn_ref[0]
    return y.astype(BF16)


def _moe_kernel(h_ref, slot_ref, wt_ref, opos_ref, *rest):
    w_up_refs = rest[:UP_SPLIT]
    b_up_ref = rest[UP_SPLIT]
    w_dn_refs = rest[UP_SPLIT + 1:UP_SPLIT + 1 + DOWN_SPLIT]
    b_dn_ref, o_ref, xs, ys = rest[UP_SPLIT + 1 + DOWN_SPLIT:]
    e = pl.program_id(1)
    member = e % MOE_GROUP
    base = pl.multiple_of(e - member, MOE_GROUP)
    gr = MOE_GRANULE

    @pl.when(e == 0)
    def _():
        o_ref[...] = jnp.zeros_like(o_ref)

    def onehot(g, weighted):
        slots = slot_ref[pl.ds(base, MOE_GROUP), g * gr:(g + 1) * gr]
        if weighted:
            wts = wt_ref[pl.ds(base, MOE_GROUP), g * gr:(g + 1) * gr]
        cap = lax.broadcasted_iota(jnp.int32, (MOE_CAP, 1), 0).astype(F32)
        blocks = []
        for m in range(MOE_GROUP):
            hit = slots[m:m + 1, :] == cap
            val = wts[m:m + 1, :] if weighted else 1.0
            blocks.append(jnp.where(hit, val, 0.0).astype(BF16))
        return jnp.concatenate(blocks, axis=0)

    @pl.when(member == 0)
    def _():
        for g in range(N_GRANULES):
            xs[g] = jnp.dot(onehot(g, False), h_ref[g * gr:(g + 1) * gr, :],
                            preferred_element_type=F32).astype(BF16)

    r0 = pl.multiple_of(member * MOE_CAP, BF16_ROWS)
    xe = jnp.concatenate([xs[g, pl.ds(r0, MOE_CAP), :] for g in range(N_GRANULES)], axis=0)
    ye = _expert(xe, w_up_refs, b_up_ref, w_dn_refs, b_dn_ref)
    for g in range(N_GRANULES):
        ys[g, pl.ds(r0, MOE_CAP), :] = ye[g * MOE_CAP:(g + 1) * MOE_CAP, :]

    @pl.when(member == MOE_GROUP - 1)
    def _():
        for g in range(N_GRANULES):
            o_ref[g * gr:(g + 1) * gr, :] += lax.dot_general(
                onehot(g, True), ys[g], (((0,), (0,)), ((), ())), preferred_element_type=F32)

    pos_row = opos_ref[pl.ds(e, 1), :]
    w_row = wt_ref[pl.ds(e, 1), :]
    count = jnp.max(pos_row).astype(jnp.int32) + 1
    n_sub = (count + (MOE_ROWS - 1)) // MOE_ROWS

    def body(sb, carry):
        slot = (lax.broadcasted_iota(jnp.int32, (MOE_ROWS, 1), 0) + sb * MOE_ROWS).astype(F32)
        hit = pos_row == slot
        gather = jnp.where(hit, 1.0, 0.0).astype(BF16)
        xg = jnp.dot(gather, h_ref[...], preferred_element_type=F32).astype(BF16)
        y = _expert(xg, w_up_refs, b_up_ref, w_dn_refs, b_dn_ref)
        scatter = jnp.where(hit, w_row, 0.0).astype(BF16)
        o_ref[...] += lax.dot_general(scatter, y, (((0,), (0,)), ((), ())),
                                      preferred_element_type=F32)
        return carry

    lax.fori_loop(0, n_sub, body, 0)


def _moe_call(h2, slot_t, w_t, opos_t, w_up, b_up, w_down, b_down):
    n_tok, d = h2.shape
    n_tiles = n_tok // MOE_TILE
    _, _, _, up_w = w_up.shape
    _, _, f, dn_w = w_down.shape
    route_spec = pl.BlockSpec((N_EXPERTS, MOE_TILE), lambda i, e: (0, i))

    def slab_spec(rows, width, n):
        return pl.BlockSpec((1, 1, rows, width), lambda i, e: (e, n, 0, 0))

    in_specs = [pl.BlockSpec((MOE_TILE, d), lambda i, e: (i, 0), pipeline_mode=pl.Buffered(1)),
                route_spec, route_spec, route_spec]
    in_specs += [slab_spec(d, up_w, n) for n in range(UP_SPLIT)]
    in_specs += [pl.BlockSpec((1, 1, UP_SPLIT * up_w), lambda i, e: (e, 0, 0))]
    in_specs += [slab_spec(f, dn_w, n) for n in range(DOWN_SPLIT)]
    in_specs += [pl.BlockSpec((1, 1, d), lambda i, e: (e, 0, 0))]
    return pl.pallas_call(
        _moe_kernel,
        grid=(n_tiles, N_EXPERTS),
        in_specs=in_specs,
        out_specs=pl.BlockSpec((MOE_TILE, d), lambda i, e: (i, 0)),
        out_shape=jax.ShapeDtypeStruct((n_tok, d), F32),
        scratch_shapes=[pltpu.VMEM((N_GRANULES, MOE_GROUP * MOE_CAP, d), BF16),
                        pltpu.VMEM((N_GRANULES, MOE_GROUP * MOE_CAP, d), BF16)],
        compiler_params=pltpu.CompilerParams(
            dimension_semantics=("arbitrary", "arbitrary"),
            vmem_limit_bytes=V7X_VMEM_BYTES - 4 * 1024 * 1024),
    )(h2, slot_t, w_t, opos_t, *([w_up] * UP_SPLIT), b_up, *([w_down] * DOWN_SPLIT), b_down)


def _final_kernel(x_ref, y_ref, mod_ref, g_ref, b_ref, o_ref):
    gate2 = mod_ref[0, 5:6, :]
    o_ref[0] = _ln(DEEPNORM_ALPHA * x_ref[0] + gate2 * y_ref[0]) * g_ref[...] + b_ref[...]


def _final_call(x1, y, mod, g, b):
    bsz, seq, d = x1.shape
    tt = FINAL_TILE
    return pl.pallas_call(
        _final_kernel,
        grid=(bsz, seq // tt),
        in_specs=[pl.BlockSpec((1, tt, d), lambda i, j: (i, j, 0)),
                  pl.BlockSpec((1, tt, d), lambda i, j: (i, j, 0)),
                  pl.BlockSpec((1, N_MOD, d), lambda i, j: (i, 0, 0)),
                  pl.BlockSpec((1, d), lambda i, j: (0, 0)),
                  pl.BlockSpec((1, d), lambda i, j: (0, 0))],
        out_specs=pl.BlockSpec((1, tt, d), lambda i, j: (i, j, 0)),
        out_shape=jax.ShapeDtypeStruct((bsz, seq, d), F32),
    )(x1, y, mod, g, b)


def _column_slabs(w, n_slabs):
    e, k, n = w.shape
    return w.astype(BF16).reshape(e, k, n_slabs, n // n_slabs).transpose(0, 2, 1, 3)


def kernel(x, c, w_ada, b_ada, w_in, b_in, conv_w, conv_b, ln_a_g, ln_a_b, w_pa, b_pa, ln_v_g, ln_v_b, w_s, b_s, w_pb, b_pb, w_out, b_out, post1_g, post1_b, w_router, b_router, w_up, b_up, w_down, b_down, post2_g, post2_b):
    bsz, seq, d = x.shape
    assert w_ada.shape[0] == DEPTH == 1
    assert seq % MOE_TILE == 0 and MOE_TILE % MOE_GRANULE == 0 and MOE_GRANULE == SEQ_TILE
    assert SEQ_TILE % SUB_TILE == 0 and SUB_TILE % CHUNK == 0 and MOE_CAP % BF16_ROWS == 0
    assert N_EXPERTS % MOE_GROUP == 0 and MOE_GROUP == V7X_SUBLANES
    l = 0
    row = lambda a: a.reshape(1, -1)
    mod = _ada_call(c, w_ada[l], b_ada[l]).reshape(bsz, N_MOD, d)
    bs_full = jnp.repeat(b_s[l].T, d // GMLP_HEADS, axis=1)
    x1, h2, slot_t, w_t, opos_t = _mixer_call(
        x, mod, w_in[l].astype(BF16), row(b_in[l]), conv_w[l], row(conv_b[l]),
        row(ln_a_g[l]), row(ln_a_b[l]), w_pa[l].astype(BF16), row(b_pa[l]),
        row(ln_v_g[l]), row(ln_v_b[l]), w_s[l], bs_full, w_pb[l].astype(BF16), row(b_pb[l]),
        w_out[l].astype(BF16), row(b_out[l]), row(post1_g[l]), row(post1_b[l]),
        w_router[l].T, b_router[l].reshape(N_EXPERTS, 1))
    y = _moe_call(h2.reshape(bsz * seq, d), slot_t, w_t, opos_t,
                  _column_slabs(w_up[l], UP_SPLIT), b_up[l].reshape(N_EXPERTS, 1, -1),
                  _column_slabs(w_down[l], DOWN_SPLIT), b_down[l].reshape(N_EXPERTS, 1, -1))
    return _final_call(x1, y.reshape(bsz, seq, d), mod, row(post2_g[l]), row(post2_b[l]))
```

```python
import math

import jax
import jax.numpy as jnp
from jax import lax
from jax.experimental import pallas as pl
from jax.experimental.pallas import tpu as pltpu

D_MODEL = 1024
CONV_WIDTH = 31
GMLP_HEADS = 8
CHUNK = 128
N_EXPERTS = 32
TOP_K = 4
D_EXPERT = 1024
SWIGLU_ALPHA = 1.702
SWIGLU_LIMIT = 7.0
LN_EPS = 1e-5
DEPTH = 1
DEEPNORM_ALPHA = (2.0 * DEPTH) ** 0.25
N_MOD = 6
N_PROJ = 6

V7X_VMEM_BYTES = 64 * 1024 * 1024
V7X_SUBLANES = 8
V7X_LANES = 128
BF16_ROWS = 2 * V7X_SUBLANES

SEQ_TILE = 512
SUB_TILE = 256
CONV_HALO = 32
CONV_ROWS = 64
CONV_COLS = 256
MOE_TILE = 2048
MOE_GRANULE = SEQ_TILE
MOE_CAP = 96
MOE_GROUP = 8
MOE_ROWS = 128
UP_SPLIT = 1
DOWN_SPLIT = 1
FINAL_TILE = 1024
N_GRANULES = MOE_TILE // MOE_GRANULE

F32 = jnp.float32
BF16 = jnp.bfloat16


def _ln(x):
    mu = jnp.mean(x, axis=-1, keepdims=True)
    xc = x - mu
    var = jnp.mean(xc * xc, axis=-1, keepdims=True)
    return xc * lax.rsqrt(var + LN_EPS)


def _sigmoid(x):
    return 1.0 / (1.0 + jnp.exp(-x))


def _gelu(x):
    return 0.5 * x * (1.0 + lax.erf(x * (1.0 / math.sqrt(2.0))))


def _ada_kernel(c_ref, w_ref, b_ref, o_ref):
    c = c_ref[...]
    cond = (c * _sigmoid(c)).astype(BF16)
    o_ref[...] = jnp.dot(cond, w_ref[...].astype(BF16), preferred_element_type=F32) + b_ref[...]


def _ada_call(c, w_ada, b_ada):
    bsz, d = c.shape
    n = w_ada.shape[1]
    bn = d
    return pl.pallas_call(
        _ada_kernel,
        grid=(n // bn,),
        in_specs=[pl.BlockSpec((bsz, d), lambda j: (0, 0)),
                  pl.BlockSpec((d, bn), lambda j: (0, j)),
                  pl.BlockSpec((1, bn), lambda j: (0, j))],
        out_specs=pl.BlockSpec((bsz, bn), lambda j: (0, j)),
        out_shape=jax.ShapeDtypeStruct((bsz, n), F32),
    )(c, w_ada, b_ada.reshape(1, n))


def _mixer_kernel(x_ref, mod_ref, w_in_ref, b_in_ref, cw_ref, cb_ref, lag_ref, lab_ref,
                  w_pa_ref, b_pa_ref, lvg_ref, lvb_ref, w_s_ref, bs_ref, w_pb_ref, b_pb_ref,
                  w_out_ref, b_out_ref, p1g_ref, p1b_ref, w_rt_ref, b_r_ref,
                  x1_ref, h2_ref, slot_ref, wt_ref, opos_ref,
                  zbuf, zsh, cbuf, mbuf, pbuf, ocnt):
    s = pl.program_id(1)
    ts = SEQ_TILE
    st = SUB_TILE
    d = D_MODEL

    @pl.when(s == 0)
    def _():
        zbuf[0:CONV_HALO, :] = jnp.zeros((CONV_HALO, d), F32)

    @pl.when((s * ts) % MOE_TILE == 0)
    def _():
        ocnt[...] = jnp.zeros_like(ocnt)

    shift1 = mod_ref[0, 0:1, :]
    scale1 = mod_ref[0, 1:2, :]
    gate1 = mod_ref[0, 2:3, :]
    shift2 = mod_ref[0, 3:4, :]
    scale2 = mod_ref[0, 4:5, :]

    row = lax.broadcasted_iota(jnp.int32, (CHUNK, CHUNK), 0)
    col = lax.broadcasted_iota(jnp.int32, (CHUNK, CHUNK), 1)
    causal = col <= row
    w_mix = [jnp.where(causal, w_s_ref[hd], 0.0).astype(BF16) for hd in range(GMLP_HEADS)]
    t_row = lax.broadcasted_iota(jnp.int32, (st, st), 0)
    t_col = lax.broadcasted_iota(jnp.int32, (st, st), 1)
    before = jnp.where(t_row < t_col, 1.0, 0.0).astype(BF16)
    e_iota = lax.broadcasted_iota(jnp.int32, (N_EXPERTS, st), 0)
    first = CONV_HALO - (CONV_WIDTH - 1)
    sh_rows = st + CONV_HALO - V7X_SUBLANES
    hd_w = d // GMLP_HEADS

    n_chains = ts // st
    n_slabs = d // CONV_COLS
    xs_in = [x_ref[0, ci * st:(ci + 1) * st, :] for ci in range(n_chains)]
    hs = [(_ln(xv) * (1.0 + scale1) + shift1).astype(BF16) for xv in xs_in]

    def proj_slab(ci, j, n):
        lo = j * d + n * CONV_COLS
        return (jnp.dot(hs[ci], w_in_ref[:, lo:lo + CONV_COLS], preferred_element_type=F32)
                + b_in_ref[:, lo:lo + CONV_COLS])

    def glu_slab(ci, n):
        zbuf[CONV_HALO + ci * st:CONV_HALO + (ci + 1) * st, n * CONV_COLS:(n + 1) * CONV_COLS] = (
            proj_slab(ci, 0, n) * _sigmoid(proj_slab(ci, 1, n)))

    def late_slab(ci, j, n):
        pbuf[ci, j - 2, :, n * CONV_COLS:(n + 1) * CONV_COLS] = proj_slab(ci, j, n)

    def conv_block(ci, c0, r0):
        r = ci * st
        if r0 == 0:
            for b in range(1, V7X_SUBLANES):
                zsh[ci, b - 1] = zbuf[r + b:r + b + sh_rows, c0:c0 + CONV_COLS]
        acc = jnp.broadcast_to(cb_ref[:, c0:c0 + CONV_COLS], (CONV_ROWS, CONV_COLS))
        for k in range(CONV_WIDTH):
            off = first + k
            b = off % V7X_SUBLANES
            a8 = off - b + r0
            if b == 0:
                tap = zbuf[r + a8:r + a8 + CONV_ROWS, c0:c0 + CONV_COLS]
            else:
                tap = zsh[ci, b - 1, a8:a8 + CONV_ROWS, :]
            acc = acc + cw_ref[k:k + 1, c0:c0 + CONV_COLS] * tap
        cbuf[ci, r0:r0 + CONV_ROWS, c0:c0 + CONV_COLS] = acc

    for n in range(n_slabs):
        glu_slab(0, n)
    mxu_items = [(glu_slab, (ci, n)) for ci in range(1, n_chains) for n in range(n_slabs)]
    mxu_items += [(late_slab, (ci, j, n)) for ci in range(n_chains)
                  for j in range(2, N_PROJ) for n in range(n_slabs)]
    blocks = [(ci, c0, r0) for ci in range(n_chains) for c0 in range(0, d, CONV_COLS)
              for r0 in range(0, st, CONV_ROWS)]
    extra = len(mxu_items) - len(blocks)
    assert 0 <= extra <= len(blocks)
    for bi, blk in enumerate(blocks):
        conv_block(*blk)
        for _ in range(2 if bi < extra else 1):
            fn, args = mxu_items.pop(0)
            fn(*args)
    assert not mxu_items

    def chain(ci, gbase, obase):
        r = ci * st
        x = xs_in[ci]
        a = _ln(cbuf[ci]) * lag_ref[...] + lab_ref[...]
        a = a * _sigmoid(a)
        y_a = jnp.dot(a.astype(BF16), w_pa_ref[...], preferred_element_type=F32) + b_pa_ref[...]

        u = _gelu(pbuf[ci, 0])
        v = (_ln(_gelu(pbuf[ci, 1])) * lvg_ref[...] + lvb_ref[...]).astype(BF16)
        for hd in range(GMLP_HEADS):
            for c in range(st // CHUNK):
                blk = jnp.dot(w_mix[hd], v[c * CHUNK:(c + 1) * CHUNK, hd * hd_w:(hd + 1) * hd_w],
                              preferred_element_type=F32)
                mbuf[ci, c * CHUNK:(c + 1) * CHUNK, hd * hd_w:(hd + 1) * hd_w] = (
                    blk + bs_ref[:, hd * hd_w:(hd + 1) * hd_w])
        gated = (u * mbuf[ci]).astype(BF16)
        y_b = jnp.dot(gated, w_pb_ref[...], preferred_element_type=F32) + b_pb_ref[...]

        merged = (_sigmoid(pbuf[ci, 2]) * y_a + _sigmoid(pbuf[ci, 3]) * y_b).astype(BF16)
        y = jnp.dot(merged, w_out_ref[...], preferred_element_type=F32) + b_out_ref[...]

        x1 = _ln(DEEPNORM_ALPHA * x + gate1 * y) * p1g_ref[...] + p1b_ref[...]
        x1_ref[0, r:r + st, :] = x1
        h2 = _ln(x1) * (1.0 + scale2) + shift2
        h2_ref[0, r:r + st, :] = h2.astype(BF16)

        logits = lax.dot_general(w_rt_ref[...], h2, (((1,), (1,)), ((), ())),
                                 precision=lax.Precision.HIGHEST,
                                 preferred_element_type=F32) + b_r_ref[...]
        work = logits
        sel = jnp.zeros((N_EXPERTS, st), F32)
        top1 = None
        for k in range(TOP_K):
            m = jnp.max(work, axis=0, keepdims=True)
            idx = jnp.min(jnp.where(work == m, e_iota, N_EXPERTS), axis=0, keepdims=True)
            pick = e_iota == idx
            if k == 0:
                top1 = m
            sel = jnp.where(pick, 1.0, sel)
            work = jnp.where(pick, -jnp.inf, work)
        chosen = sel > 0.0
        ex = jnp.where(chosen, jnp.exp(logits - top1), 0.0)
        wt_ref[:, r:r + st] = ex / jnp.sum(ex, axis=0, keepdims=True)

        rank = jnp.dot(sel.astype(BF16), before, preferred_element_type=F32) + gbase
        fits = jnp.logical_and(chosen, rank < float(MOE_CAP))
        slot_ref[:, r:r + st] = jnp.where(fits, rank, -1.0)
        over = jnp.where(jnp.logical_and(chosen, rank >= float(MOE_CAP)), 1.0, 0.0)
        orank = jnp.dot(over.astype(BF16), before, preferred_element_type=F32) + obase
        opos_ref[:, r:r + st] = jnp.where(over > 0.0, orank, -1.0)
        return (gbase + jnp.sum(sel, axis=1, keepdims=True),
                obase + jnp.sum(over, axis=1, keepdims=True))

    gbase = jnp.zeros((N_EXPERTS, 1), F32)
    obase = ocnt[:, 0:1]
    for ci in range(ts // st):
        gbase, obase = chain(ci, gbase, obase)
    ocnt[...] = jnp.broadcast_to(obase, ocnt.shape)
    zbuf[0:CONV_HALO, :] = zbuf[ts:ts + CONV_HALO, :]


def _const_spec(shape):
    nd = len(shape)
    return pl.BlockSpec(shape, lambda b, s: (0,) * nd, pipeline_mode=pl.Buffered(1))


def _mixer_call(x, mod, w_in, b_in, conv_w, conv_b, ln_a_g, ln_a_b, w_pa, b_pa, ln_v_g, ln_v_b,
                w_s, bs_full, w_pb, b_pb, w_out, b_out, post1_g, post1_b, w_rt, b_r):
    bsz, seq, d = x.shape
    ts = SEQ_TILE
    n_chains = ts // SUB_TILE
    n_s = seq // ts
    n_tok = bsz * seq
    consts = [w_in, b_in, conv_w, conv_b, ln_a_g, ln_a_b, w_pa, b_pa, ln_v_g, ln_v_b,
              w_s, bs_full, w_pb, b_pb, w_out, b_out, post1_g, post1_b, w_rt, b_r]
    in_specs = [pl.BlockSpec((1, ts, d), lambda b, s: (b, s, 0)),
                pl.BlockSpec((1, N_MOD, d), lambda b, s: (b, 0, 0))]
    in_specs += [_const_spec(a.shape) for a in consts]
    route_spec = pl.BlockSpec((N_EXPERTS, ts), lambda b, s: (0, b * n_s + s))
    route_shape = jax.ShapeDtypeStruct((N_EXPERTS, n_tok), F32)
    out_specs = [pl.BlockSpec((1, ts, d), lambda b, s: (b, s, 0)),
                 pl.BlockSpec((1, ts, d), lambda b, s: (b, s, 0)),
                 route_spec, route_spec, route_spec]
    out_shape = [jax.ShapeDtypeStruct((bsz, seq, d), F32),
                 jax.ShapeDtypeStruct((bsz, seq, d), BF16),
                 route_shape, route_shape, route_shape]
    sh_rows = SUB_TILE + CONV_HALO - V7X_SUBLANES
    return pl.pallas_call(
        _mixer_kernel,
        grid=(bsz, n_s),
        in_specs=in_specs,
        out_specs=out_specs,
        out_shape=out_shape,
        scratch_shapes=[pltpu.VMEM((CONV_HALO + ts, d), F32),
                        pltpu.VMEM((n_chains, V7X_SUBLANES - 1, sh_rows, CONV_COLS), F32),
                        pltpu.VMEM((n_chains, SUB_TILE, d), F32),
                        pltpu.VMEM((n_chains, SUB_TILE, d), F32),
                        pltpu.VMEM((n_chains, N_PROJ - 2, SUB_TILE, d), F32),
                        pltpu.VMEM((N_EXPERTS, V7X_LANES), F32)],
        compiler_params=pltpu.CompilerParams(
            dimension_semantics=("arbitrary", "arbitrary"),
            vmem_limit_bytes=V7X_VMEM_BYTES - 8 * 1024 * 1024),
    )(x, mod, *consts)


def _expert(xg, w_up_refs, b_up_ref, w_dn_refs, b_dn_ref):
    f = D_EXPERT
    zz = jnp.concatenate([jnp.dot(xg, w[0, 0], preferred_element_type=F32) for w in w_up_refs],
                         axis=1) + b_up_ref[0]
    glu = jnp.minimum(zz[:, :f], SWIGLU_LIMIT)
    lin = jnp.clip(zz[:, f:], -SWIGLU_LIMIT, SWIGLU_LIMIT)
    act = (glu * _sigmoid(SWIGLU_ALPHA * glu) * (lin + 1.0)).astype(BF16)
    y = jnp.concatenate([jnp.dot(act, w[0, 0], preferred_element_type=F32) for w in w_dn_refs],
                        axis=1) + b_dn_ref[0]
    return y.astype(BF16)


def _moe_kernel(h_ref, slot_ref, wt_ref, opos_ref, *rest):
    w_up_refs = rest[:UP_SPLIT]
    b_up_ref = rest[UP_SPLIT]
    w_dn_refs = rest[UP_SPLIT + 1:UP_SPLIT + 1 + DOWN_SPLIT]
    b_dn_ref, o_ref, xs, ys = rest[UP_SPLIT + 1 + DOWN_SPLIT:]
    e = pl.program_id(1)
    member = e % MOE_GROUP
    base = pl.multiple_of(e - member, MOE_GROUP)
    gr = MOE_GRANULE

    @pl.when(e == 0)
    def _():
        o_ref[...] = jnp.zeros_like(o_ref)

    def onehot(g, weighted):
        slots = slot_ref[pl.ds(base, MOE_GROUP), g * gr:(g + 1) * gr]
        if weighted:
            wts = wt_ref[pl.ds(base, MOE_GROUP), g * gr:(g + 1) * gr]
        cap = lax.broadcasted_iota(jnp.int32, (MOE_CAP, 1), 0).astype(F32)
        blocks = []
        for m in range(MOE_GROUP):
            hit = slots[m:m + 1, :] == cap
            val = wts[m:m + 1, :] if weighted else 1.0
            blocks.append(jnp.where(hit, val, 0.0).astype(BF16))
        return jnp.concatenate(blocks, axis=0)

    @pl.when(member == 0)
    def _():
        for g in range(N_GRANULES):
            xs[g] = jnp.dot(onehot(g, False), h_ref[g * gr:(g + 1) * gr, :],
                            preferred_element_type=F32).astype(BF16)

    r0 = pl.multiple_of(member * MOE_CAP, BF16_ROWS)
    xe = jnp.concatenate([xs[g, pl.ds(r0, MOE_CAP), :] for g in range(N_GRANULES)], axis=0)
    ye = _expert(xe, w_up_refs, b_up_ref, w_dn_refs, b_dn_ref)
    for g in range(N_GRANULES):
        ys[g, pl.ds(r0, MOE_CAP), :] = ye[g * MOE_CAP:(g + 1) * MOE_CAP, :]

    @pl.when(member == MOE_GROUP - 1)
    def _():
        for g in range(N_GRANULES):
            o_ref[g * gr:(g + 1) * gr, :] += lax.dot_general(
                onehot(g, True), ys[g], (((0,), (0,)), ((), ())), preferred_element_type=F32)

    pos_row = opos_ref[pl.ds(e, 1), :]
    w_row = wt_ref[pl.ds(e, 1), :]
    count = jnp.max(pos_row).astype(jnp.int32) + 1
    n_sub = (count + (MOE_ROWS - 1)) // MOE_ROWS

    def body(sb, carry):
        slot = (lax.broadcasted_iota(jnp.int32, (MOE_ROWS, 1), 0) + sb * MOE_ROWS).astype(F32)
        hit = pos_row == slot
        gather = jnp.where(hit, 1.0, 0.0).astype(BF16)
        xg = jnp.dot(gather, h_ref[...], preferred_element_type=F32).astype(BF16)
        y = _expert(xg, w_up_refs, b_up_ref, w_dn_refs, b_dn_ref)
        scatter = jnp.where(hit, w_row, 0.0).astype(BF16)
        o_ref[...] += lax.dot_general(scatter, y, (((0,), (0,)), ((), ())),
                                      preferred_element_type=F32)
        return carry

    lax.fori_loop(0, n_sub, body, 0)


def _moe_call(h2, slot_t, w_t, opos_t, w_up, b_up, w_down, b_down):
    n_tok, d = h2.shape
    n_tiles = n_tok // MOE_TILE
    _, _, _, up_w = w_up.shape
    _, _, f, dn_w = w_down.shape
    route_spec = pl.BlockSpec((N_EXPERTS, MOE_TILE), lambda i, e: (0, i))

    def slab_spec(rows, width, n):
        return pl.BlockSpec((1, 1, rows, width), lambda i, e: (e, n, 0, 0))

    in_specs = [pl.BlockSpec((MOE_TILE, d), lambda i, e: (i, 0), pipeline_mode=pl.Buffered(1)),
                route_spec, route_spec, route_spec]
    in_specs += [slab_spec(d, up_w, n) for n in range(UP_SPLIT)]
    in_specs += [pl.BlockSpec((1, 1, UP_SPLIT * up_w), lambda i, e: (e, 0, 0))]
    in_specs += [slab_spec(f, dn_w, n) for n in range(DOWN_SPLIT)]
    in_specs += [pl.BlockSpec((1, 1, d), lambda i, e: (e, 0, 0))]
    return pl.pallas_call(
        _moe_kernel,
        grid=(n_tiles, N_EXPERTS),
        in_specs=in_specs,
        out_specs=pl.BlockSpec((MOE_TILE, d), lambda i, e: (i, 0)),
        out_shape=jax.ShapeDtypeStruct((n_tok, d), F32),
        scratch_shapes=[pltpu.VMEM((N_GRANULES, MOE_GROUP * MOE_CAP, d), BF16),
                        pltpu.VMEM((N_GRANULES, MOE_GROUP * MOE_CAP, d), BF16)],
        compiler_params=pltpu.CompilerParams(
            dimension_semantics=("arbitrary", "arbitrary"),
            vmem_limit_bytes=V7X_VMEM_BYTES - 4 * 1024 * 1024),
    )(h2, slot_t, w_t, opos_t, *([w_up] * UP_SPLIT), b_up, *([w_down] * DOWN_SPLIT), b_down)


def _final_kernel(x_ref, y_ref, mod_ref, g_ref, b_ref, o_ref):
    gate2 = mod_ref[0, 5:6, :]
    o_ref[0] = _ln(DEEPNORM_ALPHA * x_ref[0] + gate2 * y_ref[0]) * g_ref[...] + b_ref[...]


def _final_call(x1, y, mod, g, b):
    bsz, seq, d = x1.shape
    tt = FINAL_TILE
    return pl.pallas_call(
        _final_kernel,
        grid=(bsz, seq // tt),
        in_specs=[pl.BlockSpec((1, tt, d), lambda i, j: (i, j, 0)),
                  pl.BlockSpec((1, tt, d), lambda i, j: (i, j, 0)),
                  pl.BlockSpec((1, N_MOD, d), lambda i, j: (i, 0, 0)),
                  pl.BlockSpec((1, d), lambda i, j: (0, 0)),
                  pl.BlockSpec((1, d), lambda i, j: (0, 0))],
        out_specs=pl.BlockSpec((1, tt, d), lambda i, j: (i, j, 0)),
        out_shape=jax.ShapeDtypeStruct((bsz, seq, d), F32),
    )(x1, y, mod, g, b)


def _column_slabs(w, n_slabs):
    e, k, n = w.shape
    return w.astype(BF16).reshape(e, k, n_slabs, n // n_slabs).transpose(0, 2, 1, 3)


def kernel(x, c, w_ada, b_ada, w_in, b_in, conv_w, conv_b, ln_a_g, ln_a_b, w_pa, b_pa, ln_v_g, ln_v_b, w_s, b_s, w_pb, b_pb, w_out, b_out, post1_g, post1_b, w_router, b_router, w_up, b_up, w_down, b_down, post2_g, post2_b):
    bsz, seq, d = x.shape
    assert w_ada.shape[0] == DEPTH == 1
    assert seq % MOE_TILE == 0 and MOE_TILE % MOE_GRANULE == 0 and MOE_GRANULE == SEQ_TILE
    assert SEQ_TILE % SUB_TILE == 0 and SUB_TILE % CHUNK == 0 and MOE_CAP % BF16_ROWS == 0
    assert N_EXPERTS % MOE_GROUP == 0 and MOE_GROUP == V7X_SUBLANES
    l = 0
    row = lambda a: a.reshape(1, -1)
    mod = _ada_call(c, w_ada[l], b_ada[l]).reshape(bsz, N_MOD, d)
    bs_full = jnp.repeat(b_s[l].T, d // GMLP_HEADS, axis=1)
    x1, h2, slot_t, w_t, opos_t = _mixer_call(
        x, mod, w_in[l].astype(BF16), row(b_in[l]), conv_w[l], row(conv_b[l]),
        row(ln_a_g[l]), row(ln_a_b[l]), w_pa[l].astype(BF16), row(b_pa[l]),
        row(ln_v_g[l]), row(ln_v_b[l]), w_s[l], bs_full, w_pb[l].astype(BF16), row(b_pb[l]),
        w_out[l].astype(BF16), row(b_out[l]), row(post1_g[l]), row(post1_b[l]),
        w_router[l].T, b_router[l].reshape(N_EXPERTS, 1))
    y = _moe_call(h2.reshape(bsz * seq, d), slot_t, w_t, opos_t,
                  _column_slabs(w_up[l], UP_SPLIT), b_up[l].reshape(N_EXPERTS, 1, -1),
                  _column_slabs(w_down[l], DOWN_SPLIT), b_down[l].reshape(N_EXPERTS, 1, -1))
    return _final_call(x1, y.reshape(bsz, seq, d), mod, row(post2_g[l]), row(post2_b[l]))
```

```python
import math

import jax
import jax.numpy as jnp
from jax import lax
from jax.experimental import pallas as pl
from jax.experimental.pallas import tpu as pltpu

D_MODEL = 1024
CONV_WIDTH = 31
GMLP_HEADS = 8
CHUNK = 128
N_EXPERTS = 32
TOP_K = 4
D_EXPERT = 1024
SWIGLU_ALPHA = 1.702
SWIGLU_LIMIT = 7.0
LN_EPS = 1e-5
DEPTH = 1
DEEPNORM_ALPHA = (2.0 * DEPTH) ** 0.25
N_MOD = 6
N_PROJ = 6

V7X_VMEM_BYTES = 64 * 1024 * 1024
V7X_SUBLANES = 8
V7X_LANES = 128
BF16_ROWS = 2 * V7X_SUBLANES

SEQ_TILE = 512
SUB_TILE = 256
CONV_HALO = 32
CONV_ROWS = 64
CONV_COLS = 256
MOE_TILE = 2048
MOE_GRANULE = SEQ_TILE
MOE_CAP = 96
MOE_CAP_SMALL = 64
MOE_GROUP = 8
MOE_ROWS = 128
FINAL_TILE = 1024
N_GRANULES = MOE_TILE // MOE_GRANULE

F32 = jnp.float32
BF16 = jnp.bfloat16


def _ln(x):
    mu = jnp.mean(x, axis=-1, keepdims=True)
    xc = x - mu
    var = jnp.mean(xc * xc, axis=-1, keepdims=True)
    return xc * lax.rsqrt(var + LN_EPS)


def _sigmoid(x):
    return 1.0 / (1.0 + jnp.exp(-x))


def _gelu(x):
    return 0.5 * x * (1.0 + lax.erf(x * (1.0 / math.sqrt(2.0))))


def _ada_kernel(c_ref, w_ref, b_ref, o_ref):
    c = c_ref[...]
    cond = (c * _sigmoid(c)).astype(BF16)
    o_ref[...] = jnp.dot(cond, w_ref[...].astype(BF16), preferred_element_type=F32) + b_ref[...]


def _ada_call(c, w_ada, b_ada):
    bsz, d = c.shape
    n = w_ada.shape[1]
    bn = d
    return pl.pallas_call(
        _ada_kernel,
        grid=(n // bn,),
        in_specs=[pl.BlockSpec((bsz, d), lambda j: (0, 0)),
                  pl.BlockSpec((d, bn), lambda j: (0, j)),
                  pl.BlockSpec((1, bn), lambda j: (0, j))],
        out_specs=pl.BlockSpec((bsz, bn), lambda j: (0, j)),
        out_shape=jax.ShapeDtypeStruct((bsz, n), F32),
    )(c, w_ada, b_ada.reshape(1, n))


def _mixer_kernel(x_ref, mod_ref, w_in_ref, b_in_ref, cw_ref, cb_ref, lag_ref, lab_ref,
                  w_pa_ref, b_pa_ref, lvg_ref, lvb_ref, w_s_ref, bs_ref, w_pb_ref, b_pb_ref,
                  w_out_ref, b_out_ref, p1g_ref, p1b_ref, w_rt_ref, b_r_ref,
                  x1_ref, h2_ref, slot_ref, wt_ref, opos_ref,
                  zbuf, zsh, cbuf, mbuf, pbuf, ocnt):
    s = pl.program_id(1)
    ts = SEQ_TILE
    st = SUB_TILE
    d = D_MODEL

    @pl.when(s == 0)
    def _():
        zbuf[0:CONV_HALO, :] = jnp.zeros((CONV_HALO, d), F32)

    @pl.when((s * ts) % MOE_TILE == 0)
    def _():
        ocnt[...] = jnp.zeros_like(ocnt)

    shift1 = mod_ref[0, 0:1, :]
    scale1 = mod_ref[0, 1:2, :]
    gate1 = mod_ref[0, 2:3, :]
    shift2 = mod_ref[0, 3:4, :]
    scale2 = mod_ref[0, 4:5, :]

    row = lax.broadcasted_iota(jnp.int32, (CHUNK, CHUNK), 0)
    col = lax.broadcasted_iota(jnp.int32, (CHUNK, CHUNK), 1)
    causal = col <= row
    w_mix = [jnp.where(causal, w_s_ref[hd], 0.0).astype(BF16) for hd in range(GMLP_HEADS)]
    t_row = lax.broadcasted_iota(jnp.int32, (st, st), 0)
    t_col = lax.broadcasted_iota(jnp.int32, (st, st), 1)
    before = jnp.where(t_row < t_col, 1.0, 0.0).astype(BF16)
    e_iota = lax.broadcasted_iota(jnp.int32, (N_EXPERTS, st), 0)
    first = CONV_HALO - (CONV_WIDTH - 1)
    sh_rows = st + CONV_HALO - V7X_SUBLANES
    hd_w = d // GMLP_HEADS

    n_chains = ts // st
    n_slabs = d // CONV_COLS
    xs_in = [x_ref[0, ci * st:(ci + 1) * st, :] for ci in range(n_chains)]
    hs = [(_ln(xv) * (1.0 + scale1) + shift1).astype(BF16) for xv in xs_in]

    def proj_slab(ci, j, n):
        lo = j * d + n * CONV_COLS
        return (jnp.dot(hs[ci], w_in_ref[:, lo:lo + CONV_COLS], preferred_element_type=F32)
                + b_in_ref[:, lo:lo + CONV_COLS])

    def glu_slab(ci, n):
        zbuf[CONV_HALO + ci * st:CONV_HALO + (ci + 1) * st, n * CONV_COLS:(n + 1) * CONV_COLS] = (
            proj_slab(ci, 0, n) * _sigmoid(proj_slab(ci, 1, n)))

    def late_slab(ci, j, n):
        pbuf[ci, j - 2, :, n * CONV_COLS:(n + 1) * CONV_COLS] = proj_slab(ci, j, n)

    def conv_block(ci, c0, r0):
        r = ci * st
        if r0 == 0:
            for b in range(1, V7X_SUBLANES):
                zsh[ci, b - 1] = zbuf[r + b:r + b + sh_rows, c0:c0 + CONV_COLS]
        acc = jnp.broadcast_to(cb_ref[:, c0:c0 + CONV_COLS], (CONV_ROWS, CONV_COLS))
        for k in range(CONV_WIDTH):
            off = first + k
            b = off % V7X_SUBLANES
            a8 = off - b + r0
            if b == 0:
                tap = zbuf[r + a8:r + a8 + CONV_ROWS, c0:c0 + CONV_COLS]
            else:
                tap = zsh[ci, b - 1, a8:a8 + CONV_ROWS, :]
            acc = acc + cw_ref[k:k + 1, c0:c0 + CONV_COLS] * tap
        cbuf[ci, r0:r0 + CONV_ROWS, c0:c0 + CONV_COLS] = acc

    for n in range(n_slabs):
        glu_slab(0, n)
    mxu_items = [(glu_slab, (ci, n)) for ci in range(1, n_chains) for n in range(n_slabs)]
    mxu_items += [(late_slab, (ci, j, n)) for ci in range(n_chains)
                  for j in range(2, N_PROJ) for n in range(n_slabs)]
    blocks = [(ci, c0, r0) for ci in range(n_chains) for c0 in range(0, d, CONV_COLS)
              for r0 in range(0, st, CONV_ROWS)]
    extra = len(mxu_items) - len(blocks)
    assert 0 <= extra <= len(blocks)
    for bi, blk in enumerate(blocks):
        conv_block(*blk)
        for _ in range(2 if bi < extra else 1):
            fn, args = mxu_items.pop(0)
            fn(*args)
    assert not mxu_items

    def chain(ci, gbase, obase):
        r = ci * st
        x = xs_in[ci]
        a = _ln(cbuf[ci]) * lag_ref[...] + lab_ref[...]
        a = a * _sigmoid(a)
        y_a = jnp.dot(a.astype(BF16), w_pa_ref[...], preferred_element_type=F32) + b_pa_ref[...]

        u = _gelu(pbuf[ci, 0])
        v = (_ln(_gelu(pbuf[ci, 1])) * lvg_ref[...] + lvb_ref[...]).astype(BF16)
        for hd in range(GMLP_HEADS):
            for c in range(st // CHUNK):
                blk = jnp.dot(w_mix[hd], v[c * CHUNK:(c + 1) * CHUNK, hd * hd_w:(hd + 1) * hd_w],
                              preferred_element_type=F32)
                mbuf[ci, c * CHUNK:(c + 1) * CHUNK, hd * hd_w:(hd + 1) * hd_w] = (
                    blk + bs_ref[:, hd * hd_w:(hd + 1) * hd_w])
        gated = (u * mbuf[ci]).astype(BF16)
        y_b = jnp.dot(gated, w_pb_ref[...], preferred_element_type=F32) + b_pb_ref[...]

        merged = (_sigmoid(pbuf[ci, 2]) * y_a + _sigmoid(pbuf[ci, 3]) * y_b).astype(BF16)
        y = jnp.dot(merged, w_out_ref[...], preferred_element_type=F32) + b_out_ref[...]

        x1 = _ln(DEEPNORM_ALPHA * x + gate1 * y) * p1g_ref[...] + p1b_ref[...]
        x1_ref[0, r:r + st, :] = x1
        h2 = _ln(x1) * (1.0 + scale2) + shift2
        h2_ref[0, r:r + st, :] = h2.astype(BF16)

        logits = lax.dot_general(w_rt_ref[...], h2, (((1,), (1,)), ((), ())),
                                 precision=lax.Precision.HIGHEST,
                                 preferred_element_type=F32) + b_r_ref[...]
        work = logits
        sel = jnp.zeros((N_EXPERTS, st), F32)
        top1 = None
        for k in range(TOP_K):
            m = jnp.max(work, axis=0, keepdims=True)
            idx = jnp.min(jnp.where(work == m, e_iota, N_EXPERTS), axis=0, keepdims=True)
            pick = e_iota == idx
            if k == 0:
                top1 = m
            sel = jnp.where(pick, 1.0, sel)
            work = jnp.where(pick, -jnp.inf, work)
        chosen = sel > 0.0
        ex = jnp.where(chosen, jnp.exp(logits - top1), 0.0)
        wt_ref[:, r:r + st] = ex / jnp.sum(ex, axis=0, keepdims=True)

        rank = jnp.dot(sel.astype(BF16), before, preferred_element_type=F32) + gbase
        fits = jnp.logical_and(chosen, rank < float(MOE_CAP))
        slot_ref[:, r:r + st] = jnp.where(fits, rank, -1.0)
        over = jnp.where(jnp.logical_and(chosen, rank >= float(MOE_CAP)), 1.0, 0.0)
        orank = jnp.dot(over.astype(BF16), before, preferred_element_type=F32) + obase
        opos_ref[:, r:r + st] = jnp.where(over > 0.0, orank, -1.0)
        return (gbase + jnp.sum(sel, axis=1, keepdims=True),
                obase + jnp.sum(over, axis=1, keepdims=True))

    gbase = jnp.zeros((N_EXPERTS, 1), F32)
    obase = ocnt[:, 0:1]
    for ci in range(ts // st):
        gbase, obase = chain(ci, gbase, obase)
    ocnt[...] = jnp.broadcast_to(obase, ocnt.shape)
    zbuf[0:CONV_HALO, :] = zbuf[ts:ts + CONV_HALO, :]


def _const_spec(shape):
    nd = len(shape)
    return pl.BlockSpec(shape, lambda b, s: (0,) * nd, pipeline_mode=pl.Buffered(1))


def _mixer_call(x, mod, w_in, b_in, conv_w, conv_b, ln_a_g, ln_a_b, w_pa, b_pa, ln_v_g, ln_v_b,
                w_s, bs_full, w_pb, b_pb, w_out, b_out, post1_g, post1_b, w_rt, b_r):
    bsz, seq, d = x.shape
    ts = SEQ_TILE
    n_chains = ts // SUB_TILE
    n_s = seq // ts
    n_tok = bsz * seq
    consts = [w_in, b_in, conv_w, conv_b, ln_a_g, ln_a_b, w_pa, b_pa, ln_v_g, ln_v_b,
              w_s, bs_full, w_pb, b_pb, w_out, b_out, post1_g, post1_b, w_rt, b_r]
    in_specs = [pl.BlockSpec((1, ts, d), lambda b, s: (b, s, 0)),
                pl.BlockSpec((1, N_MOD, d), lambda b, s: (b, 0, 0))]
    in_specs += [_const_spec(a.shape) for a in consts]
    route_spec = pl.BlockSpec((N_EXPERTS, ts), lambda b, s: (0, b * n_s + s))
    route_shape = jax.ShapeDtypeStruct((N_EXPERTS, n_tok), F32)
    out_specs = [pl.BlockSpec((1, ts, d), lambda b, s: (b, s, 0)),
                 pl.BlockSpec((1, ts, d), lambda b, s: (b, s, 0)),
                 route_spec, route_spec, route_spec]
    out_shape = [jax.ShapeDtypeStruct((bsz, seq, d), F32),
                 jax.ShapeDtypeStruct((bsz, seq, d), BF16),
                 route_shape, route_shape, route_shape]
    sh_rows = SUB_TILE + CONV_HALO - V7X_SUBLANES
    return pl.pallas_call(
        _mixer_kernel,
        grid=(bsz, n_s),
        in_specs=in_specs,
        out_specs=out_specs,
        out_shape=out_shape,
        scratch_shapes=[pltpu.VMEM((CONV_HALO + ts, d), F32),
                        pltpu.VMEM((n_chains, V7X_SUBLANES - 1, sh_rows, CONV_COLS), F32),
                        pltpu.VMEM((n_chains, SUB_TILE, d), F32),
                        pltpu.VMEM((n_chains, SUB_TILE, d), F32),
                        pltpu.VMEM((n_chains, N_PROJ - 2, SUB_TILE, d), F32),
                        pltpu.VMEM((N_EXPERTS, V7X_LANES), F32)],
        compiler_params=pltpu.CompilerParams(
            dimension_semantics=("arbitrary", "arbitrary"),
            vmem_limit_bytes=V7X_VMEM_BYTES - 8 * 1024 * 1024),
    )(x, mod, *consts)


def _expert(xg, w_up_ref, b_up_ref, w_dn_ref, b_dn_ref):
    f = D_EXPERT
    zz = jnp.dot(xg, w_up_ref[0], preferred_element_type=F32) + b_up_ref[0]
    glu = jnp.minimum(zz[:, :f], SWIGLU_LIMIT)
    lin = jnp.clip(zz[:, f:], -SWIGLU_LIMIT, SWIGLU_LIMIT)
    act = (glu * _sigmoid(SWIGLU_ALPHA * glu) * (lin + 1.0)).astype(BF16)
    return (jnp.dot(act, w_dn_ref[0], preferred_element_type=F32) + b_dn_ref[0]).astype(BF16)


def _moe_kernel(h_ref, slot_ref, wt_ref, opos_ref, w_up_ref, b_up_ref, w_dn_ref, b_dn_ref,
                o_ref, xs, ys):
    e = pl.program_id(1)
    member = e % MOE_GROUP
    base = pl.multiple_of(e - member, MOE_GROUP)
    gr = MOE_GRANULE

    @pl.when(e == 0)
    def _():
        o_ref[...] = jnp.zeros_like(o_ref)

    def onehot(g, weighted):
        slots = slot_ref[pl.ds(base, MOE_GROUP), g * gr:(g + 1) * gr]
        if weighted:
            wts = wt_ref[pl.ds(base, MOE_GROUP), g * gr:(g + 1) * gr]
        cap = lax.broadcasted_iota(jnp.int32, (MOE_CAP, 1), 0).astype(F32)
        blocks = []
        for m in range(MOE_GROUP):
            hit = slots[m:m + 1, :] == cap
            val = wts[m:m + 1, :] if weighted else 1.0
            blocks.append(jnp.where(hit, val, 0.0).astype(BF16))
        return jnp.concatenate(blocks, axis=0)

    @pl.when(member == 0)
    def _():
        for g in range(N_GRANULES):
            xs[g] = jnp.dot(onehot(g, False), h_ref[g * gr:(g + 1) * gr, :],
                            preferred_element_type=F32).astype(BF16)

    r0 = pl.multiple_of(member * MOE_CAP, BF16_ROWS)
    used = jnp.max(slot_ref[pl.ds(e, 1), :]).astype(jnp.int32) + 1

    def run_expert(rows):
        xe = jnp.concatenate([xs[g, pl.ds(r0, rows), :] for g in range(N_GRANULES)], axis=0)
        ye = _expert(xe, w_up_ref, b_up_ref, w_dn_ref, b_dn_ref)
        for g in range(N_GRANULES):
            ys[g, pl.ds(r0, rows), :] = ye[g * rows:(g + 1) * rows, :]
            if rows < MOE_CAP:
                rest = pl.multiple_of(r0 + rows, BF16_ROWS)
                ys[g, pl.ds(rest, MOE_CAP - rows), :] = jnp.zeros((MOE_CAP - rows, D_MODEL), BF16)

    @pl.when(used <= MOE_CAP_SMALL)
    def _():
        run_expert(MOE_CAP_SMALL)

    @pl.when(used > MOE_CAP_SMALL)
    def _():
        run_expert(MOE_CAP)

    @pl.when(member == MOE_GROUP - 1)
    def _():
        for g in range(N_GRANULES):
            o_ref[g * gr:(g + 1) * gr, :] += lax.dot_general(
                onehot(g, True), ys[g], (((0,), (0,)), ((), ())), preferred_element_type=F32)

    pos_row = opos_ref[pl.ds(e, 1), :]
    w_row = wt_ref[pl.ds(e, 1), :]
    count = jnp.max(pos_row).astype(jnp.int32) + 1
    n_sub = (count + (MOE_ROWS - 1)) // MOE_ROWS

    def body(sb, carry):
        slot = (lax.broadcasted_iota(jnp.int32, (MOE_ROWS, 1), 0) + sb * MOE_ROWS).astype(F32)
        hit = pos_row == slot
        gather = jnp.where(hit, 1.0, 0.0).astype(BF16)
        xg = jnp.dot(gather, h_ref[...], preferred_element_type=F32).astype(BF16)
        y = _expert(xg, w_up_ref, b_up_ref, w_dn_ref, b_dn_ref)
        scatter = jnp.where(hit, w_row, 0.0).astype(BF16)
        o_ref[...] += lax.dot_general(scatter, y, (((0,), (0,)), ((), ())),
                                      preferred_element_type=F32)
        return carry

    lax.fori_loop(0, n_sub, body, 0)


def _moe_call(h2, slot_t, w_t, opos_t, w_up, b_up, w_down, b_down):
    n_tok, d = h2.shape
    n_tiles = n_tok // MOE_TILE
    f2 = w_up.shape[2]
    f = w_down.shape[1]
    route_spec = pl.BlockSpec((N_EXPERTS, MOE_TILE), lambda i, e: (0, i))
    return pl.pallas_call(
        _moe_kernel,
        grid=(n_tiles, N_EXPERTS),
        in_specs=[pl.BlockSpec((MOE_TILE, d), lambda i, e: (i, 0), pipeline_mode=pl.Buffered(1)),
                  route_spec, route_spec, route_spec,
                  pl.BlockSpec((1, d, f2), lambda i, e: (e, 0, 0)),
                  pl.BlockSpec((1, 1, f2), lambda i, e: (e, 0, 0)),
                  pl.BlockSpec((1, f, d), lambda i, e: (e, 0, 0)),
                  pl.BlockSpec((1, 1, d), lambda i, e: (e, 0, 0))],
        out_specs=pl.BlockSpec((MOE_TILE, d), lambda i, e: (i, 0)),
        out_shape=jax.ShapeDtypeStruct((n_tok, d), F32),
        scratch_shapes=[pltpu.VMEM((N_GRANULES, MOE_GROUP * MOE_CAP, d), BF16),
                        pltpu.VMEM((N_GRANULES, MOE_GROUP * MOE_CAP, d), BF16)],
        compiler_params=pltpu.CompilerParams(
            dimension_semantics=("arbitrary", "arbitrary"),
            vmem_limit_bytes=V7X_VMEM_BYTES - 4 * 1024 * 1024),
    )(h2, slot_t, w_t, opos_t, w_up, b_up, w_down, b_down)


def _final_kernel(x_ref, y_ref, mod_ref, g_ref, b_ref, o_ref):
    gate2 = mod_ref[0, 5:6, :]
    o_ref[0] = _ln(DEEPNORM_ALPHA * x_ref[0] + gate2 * y_ref[0]) * g_ref[...] + b_ref[...]


def _final_call(x1, y, mod, g, b):
    bsz, seq, d = x1.shape
    tt = FINAL_TILE
    return pl.pallas_call(
        _final_kernel,
        grid=(bsz, seq // tt),
        in_specs=[pl.BlockSpec((1, tt, d), lambda i, j: (i, j, 0)),
                  pl.BlockSpec((1, tt, d), lambda i, j: (i, j, 0)),
                  pl.BlockSpec((1, N_MOD, d), lambda i, j: (i, 0, 0)),
                  pl.BlockSpec((1, d), lambda i, j: (0, 0)),
                  pl.BlockSpec((1, d), lambda i, j: (0, 0))],
        out_specs=pl.BlockSpec((1, tt, d), lambda i, j: (i, j, 0)),
        out_shape=jax.ShapeDtypeStruct((bsz, seq, d), F32),
    )(x1, y, mod, g, b)


def kernel(x, c, w_ada, b_ada, w_in, b_in, conv_w, conv_b, ln_a_g, ln_a_b, w_pa, b_pa, ln_v_g, ln_v_b, w_s, b_s, w_pb, b_pb, w_out, b_out, post1_g, post1_b, w_router, b_router, w_up, b_up, w_down, b_down, post2_g, post2_b):
    bsz, seq, d = x.shape
    assert w_ada.shape[0] == DEPTH == 1
    assert seq % MOE_TILE == 0 and MOE_TILE % MOE_GRANULE == 0 and MOE_GRANULE == SEQ_TILE
    assert SEQ_TILE % SUB_TILE == 0 and SUB_TILE % CHUNK == 0
    assert MOE_CAP % BF16_ROWS == 0 and MOE_CAP_SMALL % BF16_ROWS == 0 and MOE_CAP_SMALL < MOE_CAP
    assert N_EXPERTS % MOE_GROUP == 0 and MOE_GROUP == V7X_SUBLANES
    l = 0
    row = lambda a: a.reshape(1, -1)
    mod = _ada_call(c, w_ada[l], b_ada[l]).reshape(bsz, N_MOD, d)
    bs_full = jnp.repeat(b_s[l].T, d // GMLP_HEADS, axis=1)
    x1, h2, slot_t, w_t, opos_t = _mixer_call(
        x, mod, w_in[l].astype(BF16), row(b_in[l]), conv_w[l], row(conv_b[l]),
        row(ln_a_g[l]), row(ln_a_b[l]), w_pa[l].astype(BF16), row(b_pa[l]),
        row(ln_v_g[l]), row(ln_v_b[l]), w_s[l], bs_full, w_pb[l].astype(BF16), row(b_pb[l]),
        w_out[l].astype(BF16), row(b_out[l]), row(post1_g[l]), row(post1_b[l]),
        w_router[l].T, b_router[l].reshape(N_EXPERTS, 1))
    y = _moe_call(h2.reshape(bsz * seq, d), slot_t, w_t, opos_t,
                  w_up[l].astype(BF16), b_up[l].reshape(N_EXPERTS, 1, -1),
                  w_down[l].astype(BF16), b_down[l].reshape(N_EXPERTS, 1, -1))
    return _final_call(x1, y.reshape(bsz, seq, d), mod, row(post2_g[l]), row(post2_b[l]))
```

```python
import math

import jax
import jax.numpy as jnp
from jax import lax
from jax.experimental import pallas as pl
from jax.experimental.pallas import tpu as pltpu

D_MODEL = 1024
CONV_WIDTH = 31
GMLP_HEADS = 8
CHUNK = 128
N_EXPERTS = 32
TOP_K = 4
D_EXPERT = 1024
SWIGLU_ALPHA = 1.702
SWIGLU_LIMIT = 7.0
LN_EPS = 1e-5
DEPTH = 1
DEEPNORM_ALPHA = (2.0 * DEPTH) ** 0.25
N_MOD = 6
N_PROJ = 6

V7X_VMEM_BYTES = 64 * 1024 * 1024
V7X_SUBLANES = 8
V7X_LANES = 128
BF16_ROWS = 2 * V7X_SUBLANES

SEQ_TILE = 512
SUB_TILE = 256
CONV_HALO = 32
CONV_ROWS = 64
CONV_COLS = 256
MOE_TILE = 2048
MOE_GRANULE = SEQ_TILE
MOE_CAP = 96
MOE_CAP_SMALL = 64
MOE_GROUP = 8
MOE_ROWS = 128
FINAL_TILE = 1024
N_GRANULES = MOE_TILE // MOE_GRANULE

F32 = jnp.float32
BF16 = jnp.bfloat16


def _ln(x):
    mu = jnp.mean(x, axis=-1, keepdims=True)
    xc = x - mu
    var = jnp.mean(xc * xc, axis=-1, keepdims=True)
    return xc * lax.rsqrt(var + LN_EPS)


def _sigmoid(x):
    return 1.0 / (1.0 + jnp.exp(-x))


def _gelu(x):
    return 0.5 * x * (1.0 + lax.erf(x * (1.0 / math.sqrt(2.0))))


def _ada_kernel(c_ref, w_ref, b_ref, o_ref):
    c = c_ref[...]
    cond = (c * _sigmoid(c)).astype(BF16)
    o_ref[...] = jnp.dot(cond, w_ref[...].astype(BF16), preferred_element_type=F32) + b_ref[...]


def _ada_call(c, w_ada, b_ada):
    bsz, d = c.shape
    n = w_ada.shape[1]
    bn = d
    return pl.pallas_call(
        _ada_kernel,
        grid=(n // bn,),
        in_specs=[pl.BlockSpec((bsz, d), lambda j: (0, 0)),
                  pl.BlockSpec((d, bn), lambda j: (0, j)),
                  pl.BlockSpec((1, bn), lambda j: (0, j))],
        out_specs=pl.BlockSpec((bsz, bn), lambda j: (0, j)),
        out_shape=jax.ShapeDtypeStruct((bsz, n), F32),
    )(c, w_ada, b_ada.reshape(1, n))


def _mixer_kernel(x_ref, mod_ref, w_in_ref, b_in_ref, cw_ref, cb_ref, lag_ref, lab_ref,
                  w_pa_ref, b_pa_ref, lvg_ref, lvb_ref, w_s_ref, bs_ref, w_pb_ref, b_pb_ref,
                  w_out_ref, b_out_ref, p1g_ref, p1b_ref,
                  x1_ref,
                  zbuf, zsh, cbuf, mbuf, pbuf):
    s = pl.program_id(1)
    ts = SEQ_TILE
    st = SUB_TILE
    d = D_MODEL

    @pl.when(s == 0)
    def _():
        zbuf[0:CONV_HALO, :] = jnp.zeros((CONV_HALO, d), F32)

    shift1 = mod_ref[0, 0:1, :]
    scale1 = mod_ref[0, 1:2, :]
    gate1 = mod_ref[0, 2:3, :]

    row = lax.broadcasted_iota(jnp.int32, (CHUNK, CHUNK), 0)
    col = lax.broadcasted_iota(jnp.int32, (CHUNK, CHUNK), 1)
    causal = col <= row
    w_mix = [jnp.where(causal, w_s_ref[hd], 0.0).astype(BF16) for hd in range(GMLP_HEADS)]
    first = CONV_HALO - (CONV_WIDTH - 1)
    sh_rows = st + CONV_HALO - V7X_SUBLANES
    hd_w = d // GMLP_HEADS

    n_chains = ts // st
    n_slabs = d // CONV_COLS
    xs_in = [x_ref[0, ci * st:(ci + 1) * st, :] for ci in range(n_chains)]
    hs = [(_ln(xv) * (1.0 + scale1) + shift1).astype(BF16) for xv in xs_in]

    def proj_slab(ci, j, n):
        lo = j * d + n * CONV_COLS
        return (jnp.dot(hs[ci], w_in_ref[:, lo:lo + CONV_COLS], preferred_element_type=F32)
                + b_in_ref[:, lo:lo + CONV_COLS])

    def glu_slab(ci, n):
        zbuf[CONV_HALO + ci * st:CONV_HALO + (ci + 1) * st, n * CONV_COLS:(n + 1) * CONV_COLS] = (
            proj_slab(ci, 0, n) * _sigmoid(proj_slab(ci, 1, n)))

    def late_slab(ci, j, n):
        pbuf[ci, j - 2, :, n * CONV_COLS:(n + 1) * CONV_COLS] = proj_slab(ci, j, n)

    def conv_block(ci, c0, r0):
        r = ci * st
        if r0 == 0:
            for b in range(1, V7X_SUBLANES):
                zsh[ci, b - 1] = zbuf[r + b:r + b + sh_rows, c0:c0 + CONV_COLS]
        acc = jnp.broadcast_to(cb_ref[:, c0:c0 + CONV_COLS], (CONV_ROWS, CONV_COLS))
        for k in range(CONV_WIDTH):
            off = first + k
            b = off % V7X_SUBLANES
            a8 = off - b + r0
            if b == 0:
                tap = zbuf[r + a8:r + a8 + CONV_ROWS, c0:c0 + CONV_COLS]
            else:
                tap = zsh[ci, b - 1, a8:a8 + CONV_ROWS, :]
            acc = acc + cw_ref[k:k + 1, c0:c0 + CONV_COLS] * tap
        cbuf[ci, r0:r0 + CONV_ROWS, c0:c0 + CONV_COLS] = acc

    for n in range(n_slabs):
        glu_slab(0, n)
    mxu_items = [(glu_slab, (ci, n)) for ci in range(1, n_chains) for n in range(n_slabs)]
    mxu_items += [(late_slab, (ci, j, n)) for ci in range(n_chains)
                  for j in range(2, N_PROJ) for n in range(n_slabs)]
    blocks = [(ci, c0, r0) for ci in range(n_chains) for c0 in range(0, d, CONV_COLS)
              for r0 in range(0, st, CONV_ROWS)]
    extra = len(mxu_items) - len(blocks)
    assert 0 <= extra <= len(blocks)
    for bi, blk in enumerate(blocks):
        conv_block(*blk)
        for _ in range(2 if bi < extra else 1):
            fn, args = mxu_items.pop(0)
            fn(*args)
    assert not mxu_items

    def chain(ci):
        r = ci * st
        x = xs_in[ci]
        a = _ln(cbuf[ci]) * lag_ref[...] + lab_ref[...]
        a = a * _sigmoid(a)
        y_a = jnp.dot(a.astype(BF16), w_pa_ref[...], preferred_element_type=F32) + b_pa_ref[...]

        u = _gelu(pbuf[ci, 0])
        v = (_ln(_gelu(pbuf[ci, 1])) * lvg_ref[...] + lvb_ref[...]).astype(BF16)
        for hd in range(GMLP_HEADS):
            for c in range(st // CHUNK):
                blk = jnp.dot(w_mix[hd], v[c * CHUNK:(c + 1) * CHUNK, hd * hd_w:(hd + 1) * hd_w],
                              preferred_element_type=F32)
                mbuf[ci, c * CHUNK:(c + 1) * CHUNK, hd * hd_w:(hd + 1) * hd_w] = (
                    blk + bs_ref[:, hd * hd_w:(hd + 1) * hd_w])
        gated = (u * mbuf[ci]).astype(BF16)
        y_b = jnp.dot(gated, w_pb_ref[...], preferred_element_type=F32) + b_pb_ref[...]

        merged = (_sigmoid(pbuf[ci, 2]) * y_a + _sigmoid(pbuf[ci, 3]) * y_b).astype(BF16)
        y = jnp.dot(merged, w_out_ref[...], preferred_element_type=F32) + b_out_ref[...]

        x1 = _ln(DEEPNORM_ALPHA * x + gate1 * y) * p1g_ref[...] + p1b_ref[...]
        x1_ref[0, r:r + st, :] = x1

    for ci in range(ts // st):
        chain(ci)
    zbuf[0:CONV_HALO, :] = zbuf[ts:ts + CONV_HALO, :]


def _const_spec(shape):
    nd = len(shape)
    return pl.BlockSpec(shape, lambda b, s: (0,) * nd, pipeline_mode=pl.Buffered(1))


def _mixer_call(x, mod, w_in, b_in, conv_w, conv_b, ln_a_g, ln_a_b, w_pa, b_pa, ln_v_g, ln_v_b,
                w_s, bs_full, w_pb, b_pb, w_out, b_out, post1_g, post1_b):
    bsz, seq, d = x.shape
    ts = SEQ_TILE
    n_chains = ts // SUB_TILE
    consts = [w_in, b_in, conv_w, conv_b, ln_a_g, ln_a_b, w_pa, b_pa, ln_v_g, ln_v_b,
              w_s, bs_full, w_pb, b_pb, w_out, b_out, post1_g, post1_b]
    in_specs = [pl.BlockSpec((1, ts, d), lambda b, s: (b, s, 0)),
                pl.BlockSpec((1, N_MOD, d), lambda b, s: (b, 0, 0))]
    in_specs += [_const_spec(a.shape) for a in consts]
    sh_rows = SUB_TILE + CONV_HALO - V7X_SUBLANES
    return pl.pallas_call(
        _mixer_kernel,
        grid=(bsz, seq // ts),
        in_specs=in_specs,
        out_specs=pl.BlockSpec((1, ts, d), lambda b, s: (b, s, 0)),
        out_shape=jax.ShapeDtypeStruct((bsz, seq, d), F32),
        scratch_shapes=[pltpu.VMEM((CONV_HALO + ts, d), F32),
                        pltpu.VMEM((n_chains, V7X_SUBLANES - 1, sh_rows, CONV_COLS), F32),
                        pltpu.VMEM((n_chains, SUB_TILE, d), F32),
                        pltpu.VMEM((n_chains, SUB_TILE, d), F32),
                        pltpu.VMEM((n_chains, N_PROJ - 2, SUB_TILE, d), F32)],
        compiler_params=pltpu.CompilerParams(
            dimension_semantics=("arbitrary", "arbitrary"),
            vmem_limit_bytes=V7X_VMEM_BYTES - 8 * 1024 * 1024),
    )(x, mod, *consts)


def _route_kernel(x1_ref, mod_ref, w_rt_ref, b_r_ref, h2_ref, slot_ref, wt_ref, opos_ref, lbuf):
    gr = MOE_GRANULE
    shift2 = mod_ref[0, 3:4, :]
    scale2 = mod_ref[0, 4:5, :]
    for g in range(N_GRANULES):
        h2 = _ln(x1_ref[0, g * gr:(g + 1) * gr, :]) * (1.0 + scale2) + shift2
        h2_ref[0, g * gr:(g + 1) * gr, :] = h2.astype(BF16)
        lbuf[:, g * gr:(g + 1) * gr] = lax.dot_general(
            w_rt_ref[...], h2, (((1,), (1,)), ((), ())), precision=lax.Precision.HIGHEST,
            preferred_element_type=F32) + b_r_ref[...]

    logits = lbuf[...]
    e_iota = lax.broadcasted_iota(jnp.int32, (N_EXPERTS, MOE_TILE), 0)
    work = logits
    sel = jnp.zeros((N_EXPERTS, MOE_TILE), F32)
    top1 = None
    for k in range(TOP_K):
        m = jnp.max(work, axis=0, keepdims=True)
        idx = jnp.min(jnp.where(work == m, e_iota, N_EXPERTS), axis=0, keepdims=True)
        pick = e_iota == idx
        if k == 0:
            top1 = m
        sel = jnp.where(pick, 1.0, sel)
        work = jnp.where(pick, -jnp.inf, work)
    chosen = sel > 0.0
    ex = jnp.where(chosen, jnp.exp(logits - top1), 0.0)
    wt_ref[...] = ex / jnp.sum(ex, axis=0, keepdims=True)

    t_row = lax.broadcasted_iota(jnp.int32, (gr, gr), 0)
    t_col = lax.broadcasted_iota(jnp.int32, (gr, gr), 1)
    before = jnp.where(t_row < t_col, 1.0, 0.0).astype(BF16)
    obase = jnp.zeros((N_EXPERTS, 1), F32)
    for g in range(N_GRANULES):
        cols = slice(g * gr, (g + 1) * gr)
        sel_g = sel[:, cols]
        rank = jnp.dot(sel_g.astype(BF16), before, preferred_element_type=F32)
        picked = sel_g > 0.0
        fits = jnp.logical_and(picked, rank < float(MOE_CAP))
        slot_ref[:, cols] = jnp.where(fits, rank, -1.0)
        over = jnp.where(jnp.logical_and(picked, rank >= float(MOE_CAP)), 1.0, 0.0)
        orank = jnp.dot(over.astype(BF16), before, preferred_element_type=F32) + obase
        opos_ref[:, cols] = jnp.where(over > 0.0, orank, -1.0)
        obase = obase + jnp.sum(over, axis=1, keepdims=True)


def _route_call(x1, mod, w_rt, b_r):
    bsz, seq, d = x1.shape
    n_t = seq // MOE_TILE
    route_spec = pl.BlockSpec((N_EXPERTS, MOE_TILE), lambda b, t: (0, b * n_t + t))
    route_shape = jax.ShapeDtypeStruct((N_EXPERTS, bsz * seq), F32)
    return pl.pallas_call(
        _route_kernel,
        grid=(bsz, n_t),
        in_specs=[pl.BlockSpec((1, MOE_TILE, d), lambda b, t: (b, t, 0)),
                  pl.BlockSpec((1, N_MOD, d), lambda b, t: (b, 0, 0)),
                  pl.BlockSpec(w_rt.shape, lambda b, t: (0, 0)),
                  pl.BlockSpec(b_r.shape, lambda b, t: (0, 0))],
        out_specs=[pl.BlockSpec((1, MOE_TILE, d), lambda b, t: (b, t, 0)),
                   route_spec, route_spec, route_spec],
        out_shape=[jax.ShapeDtypeStruct((bsz, seq, d), BF16),
                   route_shape, route_shape, route_shape],
        scratch_shapes=[pltpu.VMEM((N_EXPERTS, MOE_TILE), F32)],
        compiler_params=pltpu.CompilerParams(
            dimension_semantics=("arbitrary", "arbitrary"),
            vmem_limit_bytes=V7X_VMEM_BYTES - 16 * 1024 * 1024),
    )(x1, mod, w_rt, b_r)


def _expert(xg, w_up_ref, b_up_ref, w_dn_ref, b_dn_ref):
    f = D_EXPERT
    zz = jnp.dot(xg, w_up_ref[0], preferred_element_type=F32) + b_up_ref[0]
    glu = jnp.minimum(zz[:, :f], SWIGLU_LIMIT)
    lin = jnp.clip(zz[:, f:], -SWIGLU_LIMIT, SWIGLU_LIMIT)
    act = (glu * _sigmoid(SWIGLU_ALPHA * glu) * (lin + 1.0)).astype(BF16)
    return (jnp.dot(act, w_dn_ref[0], preferred_element_type=F32) + b_dn_ref[0]).astype(BF16)


def _moe_kernel(h_ref, slot_ref, wt_ref, opos_ref, w_up_ref, b_up_ref, w_dn_ref, b_dn_ref,
                o_ref, xs, ys):
    e = pl.program_id(1)
    member = e % MOE_GROUP
    base = pl.multiple_of(e - member, MOE_GROUP)
    gr = MOE_GRANULE

    @pl.when(e == 0)
    def _():
        o_ref[...] = jnp.zeros_like(o_ref)

    def onehot(g, weighted):
        slots = slot_ref[pl.ds(base, MOE_GROUP), g * gr:(g + 1) * gr]
        if weighted:
            wts = wt_ref[pl.ds(base, MOE_GROUP), g * gr:(g + 1) * gr]
        cap = lax.broadcasted_iota(jnp.int32, (MOE_CAP, 1), 0).astype(F32)
        blocks = []
        for m in range(MOE_GROUP):
            hit = slots[m:m + 1, :] == cap
            val = wts[m:m + 1, :] if weighted else 1.0
            blocks.append(jnp.where(hit, val, 0.0).astype(BF16))
        return jnp.concatenate(blocks, axis=0)

    @pl.when(member == 0)
    def _():
        for g in range(N_GRANULES):
            xs[g] = jnp.dot(onehot(g, False), h_ref[g * gr:(g + 1) * gr, :],
                            preferred_element_type=F32).astype(BF16)

    r0 = pl.multiple_of(member * MOE_CAP, BF16_ROWS)
    used = jnp.max(slot_ref[pl.ds(e, 1), :]).astype(jnp.int32) + 1

    def run_expert(rows):
        xe = jnp.concatenate([xs[g, pl.ds(r0, rows), :] for g in range(N_GRANULES)], axis=0)
        ye = _expert(xe, w_up_ref, b_up_ref, w_dn_ref, b_dn_ref)
        for g in range(N_GRANULES):
            ys[g, pl.ds(r0, rows), :] = ye[g * rows:(g + 1) * rows, :]
            if rows < MOE_CAP:
                rest = pl.multiple_of(r0 + rows, BF16_ROWS)
                ys[g, pl.ds(rest, MOE_CAP - rows), :] = jnp.zeros((MOE_CAP - rows, D_MODEL), BF16)

    @pl.when(used <= MOE_CAP_SMALL)
    def _():
        run_expert(MOE_CAP_SMALL)

    @pl.when(used > MOE_CAP_SMALL)
    def _():
        run_expert(MOE_CAP)

    @pl.when(member == MOE_GROUP - 1)
    def _():
        for g in range(N_GRANULES):
            o_ref[g * gr:(g + 1) * gr, :] += lax.dot_general(
                onehot(g, True), ys[g], (((0,), (0,)), ((), ())), preferred_element_type=F32)

    pos_row = opos_ref[pl.ds(e, 1), :]
    w_row = wt_ref[pl.ds(e, 1), :]
    count = jnp.max(pos_row).astype(jnp.int32) + 1
    n_sub = (count + (MOE_ROWS - 1)) // MOE_ROWS

    def body(sb, carry):
        slot = (lax.broadcasted_iota(jnp.int32, (MOE_ROWS, 1), 0) + sb * MOE_ROWS).astype(F32)
        hit = pos_row == slot
        gather = jnp.where(hit, 1.0, 0.0).astype(BF16)
        xg = jnp.dot(gather, h_ref[...], preferred_element_type=F32).astype(BF16)
        y = _expert(xg, w_up_ref, b_up_ref, w_dn_ref, b_dn_ref)
        scatter = jnp.where(hit, w_row, 0.0).astype(BF16)
        o_ref[...] += lax.dot_general(scatter, y, (((0,), (0,)), ((), ())),
                                      preferred_element_type=F32)
        return carry

    lax.fori_loop(0, n_sub, body, 0)


def _moe_call(h2, slot_t, w_t, opos_t, w_up, b_up, w_down, b_down):
    n_tok, d = h2.shape
    n_tiles = n_tok // MOE_TILE
    f2 = w_up.shape[2]
    f = w_down.shape[1]
    route_spec = pl.BlockSpec((N_EXPERTS, MOE_TILE), lambda i, e: (0, i))
    return pl.pallas_call(
        _moe_kernel,
        grid=(n_tiles, N_EXPERTS),
        in_specs=[pl.BlockSpec((MOE_TILE, d), lambda i, e: (i, 0), pipeline_mode=pl.Buffered(1)),
                  route_spec, route_spec, route_spec,
                  pl.BlockSpec((1, d, f2), lambda i, e: (e, 0, 0)),
                  pl.BlockSpec((1, 1, f2), lambda i, e: (e, 0, 0)),
                  pl.BlockSpec((1, f, d), lambda i, e: (e, 0, 0)),
                  pl.BlockSpec((1, 1, d), lambda i, e: (e, 0, 0))],
        out_specs=pl.BlockSpec((MOE_TILE, d), lambda i, e: (i, 0)),
        out_shape=jax.ShapeDtypeStruct((n_tok, d), F32),
        scratch_shapes=[pltpu.VMEM((N_GRANULES, MOE_GROUP * MOE_CAP, d), BF16),
                        pltpu.VMEM((N_GRANULES, MOE_GROUP * MOE_CAP, d), BF16)],
        compiler_params=pltpu.CompilerParams(
            dimension_semantics=("arbitrary", "arbitrary"),
            vmem_limit_bytes=V7X_VMEM_BYTES - 4 * 1024 * 1024),
    )(h2, slot_t, w_t, opos_t, w_up, b_up, w_down, b_down)


def _final_kernel(x_ref, y_ref, mod_ref, g_ref, b_ref, o_ref):
    gate2 = mod_ref[0, 5:6, :]
    o_ref[0] = _ln(DEEPNORM_ALPHA * x_ref[0] + gate2 * y_ref[0]) * g_ref[...] + b_ref[...]


def _final_call(x1, y, mod, g, b):
    bsz, seq, d = x1.shape
    tt = FINAL_TILE
    return pl.pallas_call(
        _final_kernel,
        grid=(bsz, seq // tt),
        in_specs=[pl.BlockSpec((1, tt, d), lambda i, j: (i, j, 0)),
                  pl.BlockSpec((1, tt, d), lambda i, j: (i, j, 0)),
                  pl.BlockSpec((1, N_MOD, d), lambda i, j: (i, 0, 0)),
                  pl.BlockSpec((1, d), lambda i, j: (0, 0)),
                  pl.BlockSpec((1, d), lambda i, j: (0, 0))],
        out_specs=pl.BlockSpec((1, tt, d), lambda i, j: (i, j, 0)),
        out_shape=jax.ShapeDtypeStruct((bsz, seq, d), F32),
    )(x1, y, mod, g, b)


def kernel(x, c, w_ada, b_ada, w_in, b_in, conv_w, conv_b, ln_a_g, ln_a_b, w_pa, b_pa, ln_v_g, ln_v_b, w_s, b_s, w_pb, b_pb, w_out, b_out, post1_g, post1_b, w_router, b_router, w_up, b_up, w_down, b_down, post2_g, post2_b):
    bsz, seq, d = x.shape
    assert w_ada.shape[0] == DEPTH == 1
    assert seq % MOE_TILE == 0 and MOE_TILE % MOE_GRANULE == 0 and MOE_GRANULE == SEQ_TILE
    assert SEQ_TILE % SUB_TILE == 0 and SUB_TILE % CHUNK == 0
    assert MOE_CAP % BF16_ROWS == 0 and MOE_CAP_SMALL % BF16_ROWS == 0 and MOE_CAP_SMALL < MOE_CAP
    assert N_EXPERTS % MOE_GROUP == 0 and MOE_GROUP == V7X_SUBLANES
    l = 0
    row = lambda a: a.reshape(1, -1)
    mod = _ada_call(c, w_ada[l], b_ada[l]).reshape(bsz, N_MOD, d)
    bs_full = jnp.repeat(b_s[l].T, d // GMLP_HEADS, axis=1)
    x1 = _mixer_call(
        x, mod, w_in[l].astype(BF16), row(b_in[l]), conv_w[l], row(conv_b[l]),
        row(ln_a_g[l]), row(ln_a_b[l]), w_pa[l].astype(BF16), row(b_pa[l]),
        row(ln_v_g[l]), row(ln_v_b[l]), w_s[l], bs_full, w_pb[l].astype(BF16), row(b_pb[l]),
        w_out[l].astype(BF16), row(b_out[l]), row(post1_g[l]), row(post1_b[l]))
    h2, slot_t, w_t, opos_t = _route_call(x1, mod, w_router[l].T,
                                          b_router[l].reshape(N_EXPERTS, 1))
    y = _moe_call(h2.reshape(bsz * seq, d), slot_t, w_t, opos_t,
                  w_up[l].astype(BF16), b_up[l].reshape(N_EXPERTS, 1, -1),
                  w_down[l].astype(BF16), b_down[l].reshape(N_EXPERTS, 1, -1))
    return _final_call(x1, y.reshape(bsz, seq, d), mod, row(post2_g[l]), row(post2_b[l]))
```

```python
import math

import jax
import jax.numpy as jnp
from jax import lax
from jax.experimental import pallas as pl
from jax.experimental.pallas import tpu as pltpu

D_MODEL = 1024
CONV_WIDTH = 31
GMLP_HEADS = 8
CHUNK = 128
N_EXPERTS = 32
TOP_K = 4
D_EXPERT = 1024
SWIGLU_ALPHA = 1.702
SWIGLU_LIMIT = 7.0
LN_EPS = 1e-5
DEPTH = 1
DEEPNORM_ALPHA = (2.0 * DEPTH) ** 0.25
N_MOD = 6
N_PROJ = 6

V7X_VMEM_BYTES = 64 * 1024 * 1024
V7X_SUBLANES = 8
V7X_LANES = 128
BF16_ROWS = 2 * V7X_SUBLANES

SEQ_TILE = 512
SUB_TILE = 256
CONV_HALO = 32
CONV_ROWS = 64
CONV_COLS = 256
MOE_TILE = 2048
MOE_GRANULE = SEQ_TILE
MOE_CAP = 96
MOE_CAP_SMALL = 64
MOE_GROUP = 8
MOE_ROWS = 128
FINAL_TILE = 1024
N_GRANULES = MOE_TILE // MOE_GRANULE

F32 = jnp.float32
BF16 = jnp.bfloat16


def _ln(x):
    mu = jnp.mean(x, axis=-1, keepdims=True)
    xc = x - mu
    var = jnp.mean(xc * xc, axis=-1, keepdims=True)
    return xc * lax.rsqrt(var + LN_EPS)


def _sigmoid(x):
    return 1.0 / (1.0 + jnp.exp(-x))


def _gelu(x):
    return 0.5 * x * (1.0 + lax.erf(x * (1.0 / math.sqrt(2.0))))


def _ada_kernel(c_ref, w_ref, b_ref, o_ref):
    c = c_ref[...]
    cond = (c * _sigmoid(c)).astype(BF16)
    o_ref[...] = jnp.dot(cond, w_ref[...].astype(BF16), preferred_element_type=F32) + b_ref[...]


def _ada_call(c, w_ada, b_ada):
    bsz, d = c.shape
    n = w_ada.shape[1]
    bn = d
    return pl.pallas_call(
        _ada_kernel,
        grid=(n // bn,),
        in_specs=[pl.BlockSpec((bsz, d), lambda j: (0, 0)),
                  pl.BlockSpec((d, bn), lambda j: (0, j)),
                  pl.BlockSpec((1, bn), lambda j: (0, j))],
        out_specs=pl.BlockSpec((bsz, bn), lambda j: (0, j)),
        out_shape=jax.ShapeDtypeStruct((bsz, n), F32),
    )(c, w_ada, b_ada.reshape(1, n))


def _mixer_kernel(x_ref, mod_ref, w_in_ref, b_in_ref, cw_ref, cb_ref, lag_ref, lab_ref,
                  w_pa_ref, b_pa_ref, lvg_ref, lvb_ref, w_s_ref, bs_ref, w_pb_ref, b_pb_ref,
                  w_out_ref, b_out_ref, p1g_ref, p1b_ref,
                  x1_ref,
                  zbuf, zsh, cbuf, mbuf, pbuf):
    s = pl.program_id(1)
    ts = SEQ_TILE
    st = SUB_TILE
    d = D_MODEL

    @pl.when(s == 0)
    def _():
        zbuf[0:CONV_HALO, :] = jnp.zeros((CONV_HALO, d), F32)

    shift1 = mod_ref[0, 0:1, :]
    scale1 = mod_ref[0, 1:2, :]
    gate1 = mod_ref[0, 2:3, :]

    row = lax.broadcasted_iota(jnp.int32, (CHUNK, CHUNK), 0)
    col = lax.broadcasted_iota(jnp.int32, (CHUNK, CHUNK), 1)
    causal = col <= row
    w_mix = [jnp.where(causal, w_s_ref[hd], 0.0).astype(BF16) for hd in range(GMLP_HEADS)]
    first = CONV_HALO - (CONV_WIDTH - 1)
    sh_rows = st + CONV_HALO - V7X_SUBLANES
    hd_w = d // GMLP_HEADS

    n_chains = ts // st
    n_slabs = d // CONV_COLS
    xs_in = [x_ref[0, ci * st:(ci + 1) * st, :] for ci in range(n_chains)]
    hs = [(_ln(xv) * (1.0 + scale1) + shift1).astype(BF16) for xv in xs_in]

    def proj_slab(ci, j, n):
        lo = j * d + n * CONV_COLS
        return (jnp.dot(hs[ci], w_in_ref[:, lo:lo + CONV_COLS], preferred_element_type=F32)
                + b_in_ref[:, lo:lo + CONV_COLS])

    def glu_slab(ci, n):
        zbuf[CONV_HALO + ci * st:CONV_HALO + (ci + 1) * st, n * CONV_COLS:(n + 1) * CONV_COLS] = (
            proj_slab(ci, 0, n) * _sigmoid(proj_slab(ci, 1, n)))

    def late_slab(ci, j, n):
        pbuf[ci, j - 2, :, n * CONV_COLS:(n + 1) * CONV_COLS] = proj_slab(ci, j, n)

    def conv_block(ci, c0, r0):
        r = ci * st
        if r0 == 0:
            for b in range(1, V7X_SUBLANES):
                zsh[ci, b - 1] = zbuf[r + b:r + b + sh_rows, c0:c0 + CONV_COLS]
        acc = jnp.broadcast_to(cb_ref[:, c0:c0 + CONV_COLS], (CONV_ROWS, CONV_COLS))
        for k in range(CONV_WIDTH):
            off = first + k
            b = off % V7X_SUBLANES
            a8 = off - b + r0
            if b == 0:
                tap = zbuf[r + a8:r + a8 + CONV_ROWS, c0:c0 + CONV_COLS]
            else:
                tap = zsh[ci, b - 1, a8:a8 + CONV_ROWS, :]
            acc = acc + cw_ref[k:k + 1, c0:c0 + CONV_COLS] * tap
        cbuf[ci, r0:r0 + CONV_ROWS, c0:c0 + CONV_COLS] = acc

    for n in range(n_slabs):
        glu_slab(0, n)
    mxu_items = [(glu_slab, (ci, n)) for ci in range(1, n_chains) for n in range(n_slabs)]
    mxu_items += [(late_slab, (ci, j, n)) for ci in range(n_chains)
                  for j in range(2, N_PROJ) for n in range(n_slabs)]
    blocks = [(ci, c0, r0) for ci in range(n_chains) for c0 in range(0, d, CONV_COLS)
              for r0 in range(0, st, CONV_ROWS)]
    extra = len(mxu_items) - len(blocks)
    assert 0 <= extra <= len(blocks)
    for bi, blk in enumerate(blocks):
        conv_block(*blk)
        for _ in range(2 if bi < extra else 1):
            fn, args = mxu_items.pop(0)
            fn(*args)
    assert not mxu_items

    def chain(ci):
        r = ci * st
        x = xs_in[ci]
        a = _ln(cbuf[ci]) * lag_ref[...] + lab_ref[...]
        a = a * _sigmoid(a)
        y_a = jnp.dot(a.astype(BF16), w_pa_ref[...], preferred_element_type=F32) + b_pa_ref[...]

        u = _gelu(pbuf[ci, 0])
        v = (_ln(_gelu(pbuf[ci, 1])) * lvg_ref[...] + lvb_ref[...]).astype(BF16)
        for hd in range(GMLP_HEADS):
            for c in range(st // CHUNK):
                blk = jnp.dot(w_mix[hd], v[c * CHUNK:(c + 1) * CHUNK, hd * hd_w:(hd + 1) * hd_w],
                              preferred_element_type=F32)
                mbuf[ci, c * CHUNK:(c + 1) * CHUNK, hd * hd_w:(hd + 1) * hd_w] = (
                    blk + bs_ref[:, hd * hd_w:(hd + 1) * hd_w])
        gated = (u * mbuf[ci]).astype(BF16)
        y_b = jnp.dot(gated, w_pb_ref[...], preferred_element_type=F32) + b_pb_ref[...]

        merged = (_sigmoid(pbuf[ci, 2]) * y_a + _sigmoid(pbuf[ci, 3]) * y_b).astype(BF16)
        y = jnp.dot(merged, w_out_ref[...], preferred_element_type=F32) + b_out_ref[...]

        x1 = _ln(DEEPNORM_ALPHA * x + gate1 * y) * p1g_ref[...] + p1b_ref[...]
        x1_ref[0, r:r + st, :] = x1

    for ci in range(ts // st):
        chain(ci)
    zbuf[0:CONV_HALO, :] = zbuf[ts:ts + CONV_HALO, :]


def _const_spec(shape):
    nd = len(shape)
    return pl.BlockSpec(shape, lambda b, s: (0,) * nd, pipeline_mode=pl.Buffered(1))


def _mixer_call(x, mod, w_in, b_in, conv_w, conv_b, ln_a_g, ln_a_b, w_pa, b_pa, ln_v_g, ln_v_b,
                w_s, bs_full, w_pb, b_pb, w_out, b_out, post1_g, post1_b):
    bsz, seq, d = x.shape
    ts = SEQ_TILE
    n_chains = ts // SUB_TILE
    consts = [w_in, b_in, conv_w, conv_b, ln_a_g, ln_a_b, w_pa, b_pa, ln_v_g, ln_v_b,
              w_s, bs_full, w_pb, b_pb, w_out, b_out, post1_g, post1_b]
    in_specs = [pl.BlockSpec((1, ts, d), lambda b, s: (b, s, 0)),
                pl.BlockSpec((1, N_MOD, d), lambda b, s: (b, 0, 0))]
    in_specs += [_const_spec(a.shape) for a in consts]
    sh_rows = SUB_TILE + CONV_HALO - V7X_SUBLANES
    return pl.pallas_call(
        _mixer_kernel,
        grid=(bsz, seq // ts),
        in_specs=in_specs,
        out_specs=pl.BlockSpec((1, ts, d), lambda b, s: (b, s, 0)),
        out_shape=jax.ShapeDtypeStruct((bsz, seq, d), F32),
        scratch_shapes=[pltpu.VMEM((CONV_HALO + ts, d), F32),
                        pltpu.VMEM((n_chains, V7X_SUBLANES - 1, sh_rows, CONV_COLS), F32),
                        pltpu.VMEM((n_chains, SUB_TILE, d), F32),
                        pltpu.VMEM((n_chains, SUB_TILE, d), F32),
                        pltpu.VMEM((n_chains, N_PROJ - 2, SUB_TILE, d), F32)],
        compiler_params=pltpu.CompilerParams(
            dimension_semantics=("arbitrary", "arbitrary"),
            vmem_limit_bytes=V7X_VMEM_BYTES - 8 * 1024 * 1024),
    )(x, mod, *consts)


def _route_kernel(x1_ref, mod_ref, w_rt_ref, b_r_ref,
                  h2_ref, slot_ref, wt_ref, opos_ref, used_ref, over_ref, lbuf):
    gr = MOE_GRANULE
    shift2 = mod_ref[0, 3:4, :]
    scale2 = mod_ref[0, 4:5, :]
    for g in range(N_GRANULES):
        h2 = _ln(x1_ref[0, g * gr:(g + 1) * gr, :]) * (1.0 + scale2) + shift2
        h2_ref[0, g * gr:(g + 1) * gr, :] = h2.astype(BF16)
        lbuf[:, g * gr:(g + 1) * gr] = lax.dot_general(
            w_rt_ref[...], h2, (((1,), (1,)), ((), ())), precision=lax.Precision.HIGHEST,
            preferred_element_type=F32) + b_r_ref[...]

    logits = lbuf[...]
    e_iota = lax.broadcasted_iota(jnp.int32, (N_EXPERTS, MOE_TILE), 0)
    work = logits
    sel = jnp.zeros((N_EXPERTS, MOE_TILE), F32)
    top1 = None
    for k in range(TOP_K):
        m = jnp.max(work, axis=0, keepdims=True)
        idx = jnp.min(jnp.where(work == m, e_iota, N_EXPERTS), axis=0, keepdims=True)
        pick = e_iota == idx
        if k == 0:
            top1 = m
        sel = jnp.where(pick, 1.0, sel)
        work = jnp.where(pick, -jnp.inf, work)
    chosen = sel > 0.0
    ex = jnp.where(chosen, jnp.exp(logits - top1), 0.0)
    wt_ref[...] = ex / jnp.sum(ex, axis=0, keepdims=True)

    t_row = lax.broadcasted_iota(jnp.int32, (gr, gr), 0)
    t_col = lax.broadcasted_iota(jnp.int32, (gr, gr), 1)
    before = jnp.where(t_row < t_col, 1.0, 0.0).astype(BF16)
    obase = jnp.zeros((N_EXPERTS, 1), F32)
    used = jnp.zeros((N_EXPERTS, 1), F32)
    for g in range(N_GRANULES):
        cols = slice(g * gr, (g + 1) * gr)
        sel_g = sel[:, cols]
        rank = jnp.dot(sel_g.astype(BF16), before, preferred_element_type=F32)
        picked = sel_g > 0.0
        fits = jnp.logical_and(picked, rank < float(MOE_CAP))
        slot = jnp.where(fits, rank, -1.0)
        slot_ref[:, cols] = slot
        used = jnp.maximum(used, jnp.max(slot, axis=1, keepdims=True) + 1.0)
        over = jnp.where(jnp.logical_and(picked, rank >= float(MOE_CAP)), 1.0, 0.0)
        orank = jnp.dot(over.astype(BF16), before, preferred_element_type=F32) + obase
        opos_ref[:, cols] = jnp.where(over > 0.0, orank, -1.0)
        obase = obase + jnp.sum(over, axis=1, keepdims=True)
    used_ref[0] = jnp.broadcast_to(used, (N_EXPERTS, V7X_LANES))
    over_ref[0] = jnp.broadcast_to(obase, (N_EXPERTS, V7X_LANES))


def _route_call(x1, mod, w_rt, b_r):
    bsz, seq, d = x1.shape
    n_t = seq // MOE_TILE
    route_spec = pl.BlockSpec((N_EXPERTS, MOE_TILE), lambda b, t: (0, b * n_t + t))
    route_shape = jax.ShapeDtypeStruct((N_EXPERTS, bsz * seq), F32)
    tile_spec = pl.BlockSpec((1, N_EXPERTS, V7X_LANES), lambda b, t: (b * n_t + t, 0, 0))
    tile_shape = jax.ShapeDtypeStruct((bsz * n_t, N_EXPERTS, V7X_LANES), F32)
    return pl.pallas_call(
        _route_kernel,
        grid=(bsz, n_t),
        in_specs=[pl.BlockSpec((1, MOE_TILE, d), lambda b, t: (b, t, 0)),
                  pl.BlockSpec((1, N_MOD, d), lambda b, t: (b, 0, 0)),
                  pl.BlockSpec(w_rt.shape, lambda b, t: (0, 0)),
                  pl.BlockSpec(b_r.shape, lambda b, t: (0, 0))],
        out_specs=[pl.BlockSpec((1, MOE_TILE, d), lambda b, t: (b, t, 0)),
                   route_spec, route_spec, route_spec, tile_spec, tile_spec],
        out_shape=[jax.ShapeDtypeStruct((bsz, seq, d), BF16),
                   route_shape, route_shape, route_shape, tile_shape, tile_shape],
        scratch_shapes=[pltpu.VMEM((N_EXPERTS, MOE_TILE), F32)],
        compiler_params=pltpu.CompilerParams(
            dimension_semantics=("arbitrary", "arbitrary"),
            vmem_limit_bytes=V7X_VMEM_BYTES - 16 * 1024 * 1024),
    )(x1, mod, w_rt, b_r)


def _expert(xg, w_up_ref, b_up_ref, w_dn_ref, b_dn_ref):
    f = D_EXPERT
    zz = jnp.dot(xg, w_up_ref[0], preferred_element_type=F32) + b_up_ref[0]
    glu = jnp.minimum(zz[:, :f], SWIGLU_LIMIT)
    lin = jnp.clip(zz[:, f:], -SWIGLU_LIMIT, SWIGLU_LIMIT)
    act = (glu * _sigmoid(SWIGLU_ALPHA * glu) * (lin + 1.0)).astype(BF16)
    return (jnp.dot(act, w_dn_ref[0], preferred_element_type=F32) + b_dn_ref[0]).astype(BF16)


def _moe_kernel(used_ref, over_ref, h_ref, slot_ref, wt_ref, opos_ref,
                w_up_ref, b_up_ref, w_dn_ref, b_dn_ref, o_ref, xs, ys):
    e = pl.program_id(1)
    member = e % MOE_GROUP
    base = pl.multiple_of(e - member, MOE_GROUP)
    gr = MOE_GRANULE

    @pl.when(e == 0)
    def _():
        o_ref[...] = jnp.zeros_like(o_ref)

    def onehot(g, weighted):
        slots = slot_ref[pl.ds(base, MOE_GROUP), g * gr:(g + 1) * gr]
        if weighted:
            wts = wt_ref[pl.ds(base, MOE_GROUP), g * gr:(g + 1) * gr]
        cap = lax.broadcasted_iota(jnp.int32, (MOE_CAP, 1), 0).astype(F32)
        blocks = []
        for m in range(MOE_GROUP):
            hit = slots[m:m + 1, :] == cap
            val = wts[m:m + 1, :] if weighted else 1.0
            blocks.append(jnp.where(hit, val, 0.0).astype(BF16))
        return jnp.concatenate(blocks, axis=0)

    @pl.when(member == 0)
    def _():
        for g in range(N_GRANULES):
            xs[g] = jnp.dot(onehot(g, False), h_ref[g * gr:(g + 1) * gr, :],
                            preferred_element_type=F32).astype(BF16)

    r0 = pl.multiple_of(member * MOE_CAP, BF16_ROWS)
    used = used_ref[pl.program_id(0) * N_EXPERTS + e]

    def run_expert(rows):
        xe = jnp.concatenate([xs[g, pl.ds(r0, rows), :] for g in range(N_GRANULES)], axis=0)
        ye = _expert(xe, w_up_ref, b_up_ref, w_dn_ref, b_dn_ref)
        for g in range(N_GRANULES):
            ys[g, pl.ds(r0, rows), :] = ye[g * rows:(g + 1) * rows, :]
            if rows < MOE_CAP:
                rest = pl.multiple_of(r0 + rows, BF16_ROWS)
                ys[g, pl.ds(rest, MOE_CAP - rows), :] = jnp.zeros((MOE_CAP - rows, D_MODEL), BF16)

    @pl.when(used <= MOE_CAP_SMALL)
    def _():
        run_expert(MOE_CAP_SMALL)

    @pl.when(used > MOE_CAP_SMALL)
    def _():
        run_expert(MOE_CAP)

    @pl.when(member == MOE_GROUP - 1)
    def _():
        for g in range(N_GRANULES):
            o_ref[g * gr:(g + 1) * gr, :] += lax.dot_general(
                onehot(g, True), ys[g], (((0,), (0,)), ((), ())), preferred_element_type=F32)

    pos_row = opos_ref[pl.ds(e, 1), :]
    w_row = wt_ref[pl.ds(e, 1), :]
    count = over_ref[pl.program_id(0) * N_EXPERTS + e]
    n_sub = (count + (MOE_ROWS - 1)) // MOE_ROWS

    def body(sb, carry):
        slot = (lax.broadcasted_iota(jnp.int32, (MOE_ROWS, 1), 0) + sb * MOE_ROWS).astype(F32)
        hit = pos_row == slot
        gather = jnp.where(hit, 1.0, 0.0).astype(BF16)
        xg = jnp.dot(gather, h_ref[...], preferred_element_type=F32).astype(BF16)
        y = _expert(xg, w_up_ref, b_up_ref, w_dn_ref, b_dn_ref)
        scatter = jnp.where(hit, w_row, 0.0).astype(BF16)
        o_ref[...] += lax.dot_general(scatter, y, (((0,), (0,)), ((), ())),
                                      preferred_element_type=F32)
        return carry

    lax.fori_loop(0, n_sub, body, 0)


def _moe_call(used, over, h2, slot_t, w_t, opos_t, w_up, b_up, w_down, b_down):
    n_tok, d = h2.shape
    n_tiles = n_tok // MOE_TILE
    f2 = w_up.shape[2]
    f = w_down.shape[1]
    route_spec = pl.BlockSpec((N_EXPERTS, MOE_TILE), lambda i, e, *_: (0, i))
    grid_spec = pltpu.PrefetchScalarGridSpec(
        num_scalar_prefetch=2,
        grid=(n_tiles, N_EXPERTS),
        in_specs=[pl.BlockSpec((MOE_TILE, d), lambda i, e, *_: (i, 0),
                               pipeline_mode=pl.Buffered(1)),
                  route_spec, route_spec, route_spec,
                  pl.BlockSpec((1, d, f2), lambda i, e, *_: (e, 0, 0)),
                  pl.BlockSpec((1, 1, f2), lambda i, e, *_: (e, 0, 0)),
                  pl.BlockSpec((1, f, d), lambda i, e, *_: (e, 0, 0)),
                  pl.BlockSpec((1, 1, d), lambda i, e, *_: (e, 0, 0))],
        out_specs=pl.BlockSpec((MOE_TILE, d), lambda i, e, *_: (i, 0)),
        scratch_shapes=[pltpu.VMEM((N_GRANULES, MOE_GROUP * MOE_CAP, d), BF16),
                        pltpu.VMEM((N_GRANULES, MOE_GROUP * MOE_CAP, d), BF16)])
    return pl.pallas_call(
        _moe_kernel,
        grid_spec=grid_spec,
        out_shape=jax.ShapeDtypeStruct((n_tok, d), F32),
        compiler_params=pltpu.CompilerParams(
            dimension_semantics=("arbitrary", "arbitrary"),
            vmem_limit_bytes=V7X_VMEM_BYTES - 4 * 1024 * 1024),
    )(used, over, h2, slot_t, w_t, opos_t, w_up, b_up, w_down, b_down)


def _final_kernel(x_ref, y_ref, mod_ref, g_ref, b_ref, o_ref):
    gate2 = mod_ref[0, 5:6, :]
    o_ref[0] = _ln(DEEPNORM_ALPHA * x_ref[0] + gate2 * y_ref[0]) * g_ref[...] + b_ref[...]


def _final_call(x1, y, mod, g, b):
    bsz, seq, d = x1.shape
    tt = FINAL_TILE
    return pl.pallas_call(
        _final_kernel,
        grid=(bsz, seq // tt),
        in_specs=[pl.BlockSpec((1, tt, d), lambda i, j: (i, j, 0)),
                  pl.BlockSpec((1, tt, d), lambda i, j: (i, j, 0)),
                  pl.BlockSpec((1, N_MOD, d), lambda i, j: (i, 0, 0)),
                  pl.BlockSpec((1, d), lambda i, j: (0, 0)),
                  pl.BlockSpec((1, d), lambda i, j: (0, 0))],
        out_specs=pl.BlockSpec((1, tt, d), lambda i, j: (i, j, 0)),
        out_shape=jax.ShapeDtypeStruct((bsz, seq, d), F32),
    )(x1, y, mod, g, b)


def kernel(x, c, w_ada, b_ada, w_in, b_in, conv_w, conv_b, ln_a_g, ln_a_b, w_pa, b_pa, ln_v_g, ln_v_b, w_s, b_s, w_pb, b_pb, w_out, b_out, post1_g, post1_b, w_router, b_router, w_up, b_up, w_down, b_down, post2_g, post2_b):
    bsz, seq, d = x.shape
    assert w_ada.shape[0] == DEPTH == 1
    assert seq % MOE_TILE == 0 and MOE_TILE % MOE_GRANULE == 0 and MOE_GRANULE == SEQ_TILE
    assert SEQ_TILE % SUB_TILE == 0 and SUB_TILE % CHUNK == 0
    assert MOE_CAP % BF16_ROWS == 0 and MOE_CAP_SMALL % BF16_ROWS == 0 and MOE_CAP_SMALL < MOE_CAP
    assert N_EXPERTS % MOE_GROUP == 0 and MOE_GROUP == V7X_SUBLANES
    l = 0
    row = lambda a: a.reshape(1, -1)
    mod = _ada_call(c, w_ada[l], b_ada[l]).reshape(bsz, N_MOD, d)
    bs_full = jnp.repeat(b_s[l].T, d // GMLP_HEADS, axis=1)
    x1 = _mixer_call(
        x, mod, w_in[l].astype(BF16), row(b_in[l]), conv_w[l], row(conv_b[l]),
        row(ln_a_g[l]), row(ln_a_b[l]), w_pa[l].astype(BF16), row(b_pa[l]),
        row(ln_v_g[l]), row(ln_v_b[l]), w_s[l], bs_full, w_pb[l].astype(BF16), row(b_pb[l]),
        w_out[l].astype(BF16), row(b_out[l]), row(post1_g[l]), row(post1_b[l]))
    h2, slot_t, w_t, opos_t, used, over = _route_call(x1, mod, w_router[l].T,
                                                      b_router[l].reshape(N_EXPERTS, 1))
    as_scalars = lambda a: a[:, :, 0].astype(jnp.int32).reshape(-1)
    y = _moe_call(as_scalars(used), as_scalars(over), h2.reshape(bsz * seq, d), slot_t, w_t, opos_t,
                  w_up[l].astype(BF16), b_up[l].reshape(N_EXPERTS, 1, -1),
                  w_down[l].astype(BF16), b_down[l].reshape(N_EXPERTS, 1, -1))
    return _final_call(x1, y.reshape(bsz, seq, d), mod, row(post2_g[l]), row(post2_b[l]))
```

```python
import math

import jax
import jax.numpy as jnp
from jax import lax
from jax.experimental import pallas as pl
from jax.experimental.pallas import tpu as pltpu

D_MODEL = 1024
CONV_WIDTH = 31
GMLP_HEADS = 8
CHUNK = 128
N_EXPERTS = 32
TOP_K = 4
D_EXPERT = 1024
SWIGLU_ALPHA = 1.702
SWIGLU_LIMIT = 7.0
LN_EPS = 1e-5
DEPTH = 1
DEEPNORM_ALPHA = (2.0 * DEPTH) ** 0.25
N_MOD = 6
N_PROJ = 6

V7X_VMEM_BYTES = 64 * 1024 * 1024
V7X_SUBLANES = 8
V7X_LANES = 128
BF16_ROWS = 2 * V7X_SUBLANES

SEQ_TILE = 512
SUB_TILE = 256
CONV_HALO = 32
CONV_ROWS = 64
CONV_COLS = 256
MOE_TILE = 2048
MOE_GRANULE = SEQ_TILE
MOE_CAP = 96
MOE_FFN_ROWS = (48, 64, 80, MOE_CAP)
MOE_GROUP = 8
MOE_ROWS = 128
FINAL_TILE = 1024
N_GRANULES = MOE_TILE // MOE_GRANULE

F32 = jnp.float32
BF16 = jnp.bfloat16


def _ln(x):
    mu = jnp.mean(x, axis=-1, keepdims=True)
    xc = x - mu
    var = jnp.mean(xc * xc, axis=-1, keepdims=True)
    return xc * lax.rsqrt(var + LN_EPS)


def _sigmoid(x):
    return 1.0 / (1.0 + jnp.exp(-x))


def _gelu(x):
    return 0.5 * x * (1.0 + lax.erf(x * (1.0 / math.sqrt(2.0))))


def _ada_kernel(c_ref, w_ref, b_ref, o_ref):
    c = c_ref[...]
    cond = (c * _sigmoid(c)).astype(BF16)
    o_ref[...] = jnp.dot(cond, w_ref[...].astype(BF16), preferred_element_type=F32) + b_ref[...]


def _ada_call(c, w_ada, b_ada):
    bsz, d = c.shape
    n = w_ada.shape[1]
    bn = d
    return pl.pallas_call(
        _ada_kernel,
        grid=(n // bn,),
        in_specs=[pl.BlockSpec((bsz, d), lambda j: (0, 0)),
                  pl.BlockSpec((d, bn), lambda j: (0, j)),
                  pl.BlockSpec((1, bn), lambda j: (0, j))],
        out_specs=pl.BlockSpec((bsz, bn), lambda j: (0, j)),
        out_shape=jax.ShapeDtypeStruct((bsz, n), F32),
    )(c, w_ada, b_ada.reshape(1, n))


def _mixer_kernel(x_ref, mod_ref, w_in_ref, b_in_ref, cw_ref, cb_ref, lag_ref, lab_ref,
                  w_pa_ref, b_pa_ref, lvg_ref, lvb_ref, w_s_ref, bs_ref, w_pb_ref, b_pb_ref,
                  w_out_ref, b_out_ref, p1g_ref, p1b_ref,
                  x1_ref,
                  zbuf, zsh, cbuf, mbuf, pbuf):
    s = pl.program_id(1)
    ts = SEQ_TILE
    st = SUB_TILE
    d = D_MODEL

    @pl.when(s == 0)
    def _():
        zbuf[0:CONV_HALO, :] = jnp.zeros((CONV_HALO, d), F32)

    shift1 = mod_ref[0, 0:1, :]
    scale1 = mod_ref[0, 1:2, :]
    gate1 = mod_ref[0, 2:3, :]

    row = lax.broadcasted_iota(jnp.int32, (CHUNK, CHUNK), 0)
    col = lax.broadcasted_iota(jnp.int32, (CHUNK, CHUNK), 1)
    causal = col <= row
    w_mix = [jnp.where(causal, w_s_ref[hd], 0.0).astype(BF16) for hd in range(GMLP_HEADS)]
    first = CONV_HALO - (CONV_WIDTH - 1)
    sh_rows = st + CONV_HALO - V7X_SUBLANES
    hd_w = d // GMLP_HEADS

    n_chains = ts // st
    n_slabs = d // CONV_COLS
    xs_in = [x_ref[0, ci * st:(ci + 1) * st, :] for ci in range(n_chains)]
    hs = [(_ln(xv) * (1.0 + scale1) + shift1).astype(BF16) for xv in xs_in]

    def proj_slab(ci, j, n):
        lo = j * d + n * CONV_COLS
        return (jnp.dot(hs[ci], w_in_ref[:, lo:lo + CONV_COLS], preferred_element_type=F32)
                + b_in_ref[:, lo:lo + CONV_COLS])

    def glu_slab(ci, n):
        zbuf[CONV_HALO + ci * st:CONV_HALO + (ci + 1) * st, n * CONV_COLS:(n + 1) * CONV_COLS] = (
            proj_slab(ci, 0, n) * _sigmoid(proj_slab(ci, 1, n)))

    def late_slab(ci, j, n):
        pbuf[ci, j - 2, :, n * CONV_COLS:(n + 1) * CONV_COLS] = proj_slab(ci, j, n)

    def conv_block(ci, c0, r0):
        r = ci * st
        if r0 == 0:
            for b in range(1, V7X_SUBLANES):
                zsh[ci, b - 1] = zbuf[r + b:r + b + sh_rows, c0:c0 + CONV_COLS]
        acc = jnp.broadcast_to(cb_ref[:, c0:c0 + CONV_COLS], (CONV_ROWS, CONV_COLS))
        for k in range(CONV_WIDTH):
            off = first + k
            b = off % V7X_SUBLANES
            a8 = off - b + r0
            if b == 0:
                tap = zbuf[r + a8:r + a8 + CONV_ROWS, c0:c0 + CONV_COLS]
            else:
                tap = zsh[ci, b - 1, a8:a8 + CONV_ROWS, :]
            acc = acc + cw_ref[k:k + 1, c0:c0 + CONV_COLS] * tap
        cbuf[ci, r0:r0 + CONV_ROWS, c0:c0 + CONV_COLS] = acc

    for n in range(n_slabs):
        glu_slab(0, n)
    mxu_items = [(glu_slab, (ci, n)) for ci in range(1, n_chains) for n in range(n_slabs)]
    mxu_items += [(late_slab, (ci, j, n)) for ci in range(n_chains)
                  for j in range(2, N_PROJ) for n in range(n_slabs)]
    blocks = [(ci, c0, r0) for ci in range(n_chains) for c0 in range(0, d, CONV_COLS)
              for r0 in range(0, st, CONV_ROWS)]
    extra = len(mxu_items) - len(blocks)
    assert 0 <= extra <= len(blocks)
    for bi, blk in enumerate(blocks):
        conv_block(*blk)
        for _ in range(2 if bi < extra else 1):
            fn, args = mxu_items.pop(0)
            fn(*args)
    assert not mxu_items

    def chain(ci):
        r = ci * st
        x = xs_in[ci]
        a = _ln(cbuf[ci]) * lag_ref[...] + lab_ref[...]
        a = a * _sigmoid(a)
        y_a = jnp.dot(a.astype(BF16), w_pa_ref[...], preferred_element_type=F32) + b_pa_ref[...]

        u = _gelu(pbuf[ci, 0])
        v = (_ln(_gelu(pbuf[ci, 1])) * lvg_ref[...] + lvb_ref[...]).astype(BF16)
        for hd in range(GMLP_HEADS):
            for c in range(st // CHUNK):
                blk = jnp.dot(w_mix[hd], v[c * CHUNK:(c + 1) * CHUNK, hd * hd_w:(hd + 1) * hd_w],
                              preferred_element_type=F32)
                mbuf[ci, c * CHUNK:(c + 1) * CHUNK, hd * hd_w:(hd + 1) * hd_w] = (
                    blk + bs_ref[:, hd * hd_w:(hd + 1) * hd_w])
        gated = (u * mbuf[ci]).astype(BF16)
        y_b = jnp.dot(gated, w_pb_ref[...], preferred_element_type=F32) + b_pb_ref[...]

        merged = (_sigmoid(pbuf[ci, 2]) * y_a + _sigmoid(pbuf[ci, 3]) * y_b).astype(BF16)
        y = jnp.dot(merged, w_out_ref[...], preferred_element_type=F32) + b_out_ref[...]

        x1 = _ln(DEEPNORM_ALPHA * x + gate1 * y) * p1g_ref[...] + p1b_ref[...]
        x1_ref[0, r:r + st, :] = x1

    for ci in range(ts // st):
        chain(ci)
    zbuf[0:CONV_HALO, :] = zbuf[ts:ts + CONV_HALO, :]


def _const_spec(shape):
    nd = len(shape)
    return pl.BlockSpec(shape, lambda b, s: (0,) * nd, pipeline_mode=pl.Buffered(1))


def _mixer_call(x, mod, w_in, b_in, conv_w, conv_b, ln_a_g, ln_a_b, w_pa, b_pa, ln_v_g, ln_v_b,
                w_s, bs_full, w_pb, b_pb, w_out, b_out, post1_g, post1_b):
    bsz, seq, d = x.shape
    ts = SEQ_TILE
    n_chains = ts // SUB_TILE
    consts = [w_in, b_in, conv_w, conv_b, ln_a_g, ln_a_b, w_pa, b_pa, ln_v_g, ln_v_b,
              w_s, bs_full, w_pb, b_pb, w_out, b_out, post1_g, post1_b]
    in_specs = [pl.BlockSpec((1, ts, d), lambda b, s: (b, s, 0)),
                pl.BlockSpec((1, N_MOD, d), lambda b, s: (b, 0, 0))]
    in_specs += [_const_spec(a.shape) for a in consts]
    sh_rows = SUB_TILE + CONV_HALO - V7X_SUBLANES
    return pl.pallas_call(
        _mixer_kernel,
        grid=(bsz, seq // ts),
        in_specs=in_specs,
        out_specs=pl.BlockSpec((1, ts, d), lambda b, s: (b, s, 0)),
        out_shape=jax.ShapeDtypeStruct((bsz, seq, d), F32),
        scratch_shapes=[pltpu.VMEM((CONV_HALO + ts, d), F32),
                        pltpu.VMEM((n_chains, V7X_SUBLANES - 1, sh_rows, CONV_COLS), F32),
                        pltpu.VMEM((n_chains, SUB_TILE, d), F32),
                        pltpu.VMEM((n_chains, SUB_TILE, d), F32),
                        pltpu.VMEM((n_chains, N_PROJ - 2, SUB_TILE, d), F32)],
        compiler_params=pltpu.CompilerParams(
            dimension_semantics=("arbitrary", "arbitrary"),
            vmem_limit_bytes=V7X_VMEM_BYTES - 8 * 1024 * 1024),
    )(x, mod, *consts)


def _route_kernel(x1_ref, mod_ref, w_rt_ref, b_r_ref,
                  h2_ref, slot_ref, wt_ref, opos_ref, used_ref, over_ref, lbuf):
    gr = MOE_GRANULE
    shift2 = mod_ref[0, 3:4, :]
    scale2 = mod_ref[0, 4:5, :]
    for g in range(N_GRANULES):
        h2 = _ln(x1_ref[0, g * gr:(g + 1) * gr, :]) * (1.0 + scale2) + shift2
        h2_ref[0, g * gr:(g + 1) * gr, :] = h2.astype(BF16)
        lbuf[:, g * gr:(g + 1) * gr] = lax.dot_general(
            w_rt_ref[...], h2, (((1,), (1,)), ((), ())), precision=lax.Precision.HIGHEST,
            preferred_element_type=F32) + b_r_ref[...]

    logits = lbuf[...]
    e_iota = lax.broadcasted_iota(jnp.int32, (N_EXPERTS, MOE_TILE), 0)
    work = logits
    sel = jnp.zeros((N_EXPERTS, MOE_TILE), F32)
    top1 = None
    for k in range(TOP_K):
        m = jnp.max(work, axis=0, keepdims=True)
        idx = jnp.min(jnp.where(work == m, e_iota, N_EXPERTS), axis=0, keepdims=True)
        pick = e_iota == idx
        if k == 0:
            top1 = m
        sel = jnp.where(pick, 1.0, sel)
        work = jnp.where(pick, -jnp.inf, work)
    chosen = sel > 0.0
    ex = jnp.where(chosen, jnp.exp(logits - top1), 0.0)
    wt_ref[...] = ex / jnp.sum(ex, axis=0, keepdims=True)

    t_row = lax.broadcasted_iota(jnp.int32, (gr, gr), 0)
    t_col = lax.broadcasted_iota(jnp.int32, (gr, gr), 1)
    before = jnp.where(t_row < t_col, 1.0, 0.0).astype(BF16)
    obase = jnp.zeros((N_EXPERTS, 1), F32)
    used = jnp.zeros((N_EXPERTS, 1), F32)
    for g in range(N_GRANULES):
        cols = slice(g * gr, (g + 1) * gr)
        sel_g = sel[:, cols]
        rank = jnp.dot(sel_g.astype(BF16), before, preferred_element_type=F32)
        picked = sel_g > 0.0
        fits = jnp.logical_and(picked, rank < float(MOE_CAP))
        slot = jnp.where(fits, rank, -1.0)
        slot_ref[:, cols] = slot
        used = jnp.maximum(used, jnp.max(slot, axis=1, keepdims=True) + 1.0)
        over = jnp.where(jnp.logical_and(picked, rank >= float(MOE_CAP)), 1.0, 0.0)
        orank = jnp.dot(over.astype(BF16), before, preferred_element_type=F32) + obase
        opos_ref[:, cols] = jnp.where(over > 0.0, orank, -1.0)
        obase = obase + jnp.sum(over, axis=1, keepdims=True)
    used_ref[0] = jnp.broadcast_to(used, (N_EXPERTS, V7X_LANES))
    over_ref[0] = jnp.broadcast_to(obase, (N_EXPERTS, V7X_LANES))


def _route_call(x1, mod, w_rt, b_r):
    bsz, seq, d = x1.shape
    n_t = seq // MOE_TILE
    route_spec = pl.BlockSpec((N_EXPERTS, MOE_TILE), lambda b, t: (0, b * n_t + t))
    route_shape = jax.ShapeDtypeStruct((N_EXPERTS, bsz * seq), F32)
    tile_spec = pl.BlockSpec((1, N_EXPERTS, V7X_LANES), lambda b, t: (b * n_t + t, 0, 0))
    tile_shape = jax.ShapeDtypeStruct((bsz * n_t, N_EXPERTS, V7X_LANES), F32)
    return pl.pallas_call(
        _route_kernel,
        grid=(bsz, n_t),
        in_specs=[pl.BlockSpec((1, MOE_TILE, d), lambda b, t: (b, t, 0)),
                  pl.BlockSpec((1, N_MOD, d), lambda b, t: (b, 0, 0)),
                  pl.BlockSpec(w_rt.shape, lambda b, t: (0, 0)),
                  pl.BlockSpec(b_r.shape, lambda b, t: (0, 0))],
        out_specs=[pl.BlockSpec((1, MOE_TILE, d), lambda b, t: (b, t, 0)),
                   route_spec, route_spec, route_spec, tile_spec, tile_spec],
        out_shape=[jax.ShapeDtypeStruct((bsz, seq, d), BF16),
                   route_shape, route_shape, route_shape, tile_shape, tile_shape],
        scratch_shapes=[pltpu.VMEM((N_EXPERTS, MOE_TILE), F32)],
        compiler_params=pltpu.CompilerParams(
            dimension_semantics=("arbitrary", "arbitrary"),
            vmem_limit_bytes=V7X_VMEM_BYTES - 16 * 1024 * 1024),
    )(x1, mod, w_rt, b_r)


def _expert(xg, w_up_ref, b_up_ref, w_dn_ref, b_dn_ref):
    f = D_EXPERT
    zz = jnp.dot(xg, w_up_ref[0], preferred_element_type=F32) + b_up_ref[0]
    glu = jnp.minimum(zz[:, :f], SWIGLU_LIMIT)
    lin = jnp.clip(zz[:, f:], -SWIGLU_LIMIT, SWIGLU_LIMIT)
    act = (glu * _sigmoid(SWIGLU_ALPHA * glu) * (lin + 1.0)).astype(BF16)
    return (jnp.dot(act, w_dn_ref[0], preferred_element_type=F32) + b_dn_ref[0]).astype(BF16)


def _moe_kernel(used_ref, over_ref, h_ref, slot_ref, wt_ref, opos_ref,
                w_up_ref, b_up_ref, w_dn_ref, b_dn_ref, o_ref, xs, ys):
    e = pl.program_id(1)
    member = e % MOE_GROUP
    base = pl.multiple_of(e - member, MOE_GROUP)
    gr = MOE_GRANULE

    @pl.when(e == 0)
    def _():
        o_ref[...] = jnp.zeros_like(o_ref)

    def onehot(g, weighted):
        slots = slot_ref[pl.ds(base, MOE_GROUP), g * gr:(g + 1) * gr]
        if weighted:
            wts = wt_ref[pl.ds(base, MOE_GROUP), g * gr:(g + 1) * gr]
        cap = lax.broadcasted_iota(jnp.int32, (MOE_CAP, 1), 0).astype(F32)
        blocks = []
        for m in range(MOE_GROUP):
            hit = slots[m:m + 1, :] == cap
            val = wts[m:m + 1, :] if weighted else 1.0
            blocks.append(jnp.where(hit, val, 0.0).astype(BF16))
        return jnp.concatenate(blocks, axis=0)

    @pl.when(member == 0)
    def _():
        for g in range(N_GRANULES):
            xs[g] = jnp.dot(onehot(g, False), h_ref[g * gr:(g + 1) * gr, :],
                            preferred_element_type=F32).astype(BF16)

    r0 = pl.multiple_of(member * MOE_CAP, BF16_ROWS)
    used = used_ref[pl.program_id(0) * N_EXPERTS + e]

    def run_expert(rows):
        xe = jnp.concatenate([xs[g, pl.ds(r0, rows), :] for g in range(N_GRANULES)], axis=0)
        ye = _expert(xe, w_up_ref, b_up_ref, w_dn_ref, b_dn_ref)
        for g in range(N_GRANULES):
            ys[g, pl.ds(r0, rows), :] = ye[g * rows:(g + 1) * rows, :]
            if rows < MOE_CAP:
                rest = pl.multiple_of(r0 + rows, BF16_ROWS)
                ys[g, pl.ds(rest, MOE_CAP - rows), :] = jnp.zeros((MOE_CAP - rows, D_MODEL), BF16)

    for below, rows in zip((-1,) + MOE_FFN_ROWS[:-1], MOE_FFN_ROWS):
        pl.when(jnp.logical_and(used > below, used <= rows))(lambda rows=rows: run_expert(rows))

    @pl.when(member == MOE_GROUP - 1)
    def _():
        for g in range(N_GRANULES):
            o_ref[g * gr:(g + 1) * gr, :] += lax.dot_general(
                onehot(g, True), ys[g], (((0,), (0,)), ((), ())), preferred_element_type=F32)

    pos_row = opos_ref[pl.ds(e, 1), :]
    w_row = wt_ref[pl.ds(e, 1), :]
    count = over_ref[pl.program_id(0) * N_EXPERTS + e]
    n_sub = (count + (MOE_ROWS - 1)) // MOE_ROWS

    def body(sb, carry):
        slot = (lax.broadcasted_iota(jnp.int32, (MOE_ROWS, 1), 0) + sb * MOE_ROWS).astype(F32)
        hit = pos_row == slot
        gather = jnp.where(hit, 1.0, 0.0).astype(BF16)
        xg = jnp.dot(gather, h_ref[...], preferred_element_type=F32).astype(BF16)
        y = _expert(xg, w_up_ref, b_up_ref, w_dn_ref, b_dn_ref)
        scatter = jnp.where(hit, w_row, 0.0).astype(BF16)
        o_ref[...] += lax.dot_general(scatter, y, (((0,), (0,)), ((), ())),
                                      preferred_element_type=F32)
        return carry

    lax.fori_loop(0, n_sub, body, 0)


def _moe_call(used, over, h2, slot_t, w_t, opos_t, w_up, b_up, w_down, b_down):
    n_tok, d = h2.shape
    n_tiles = n_tok // MOE_TILE
    f2 = w_up.shape[2]
    f = w_down.shape[1]
    route_spec = pl.BlockSpec((N_EXPERTS, MOE_TILE), lambda i, e, *_: (0, i))
    grid_spec = pltpu.PrefetchScalarGridSpec(
        num_scalar_prefetch=2,
        grid=(n_tiles, N_EXPERTS),
        in_specs=[pl.BlockSpec((MOE_TILE, d), lambda i, e, *_: (i, 0),
                               pipeline_mode=pl.Buffered(1)),
                  route_spec, route_spec, route_spec,
                  pl.BlockSpec((1, d, f2), lambda i, e, *_: (e, 0, 0)),
                  pl.BlockSpec((1, 1, f2), lambda i, e, *_: (e, 0, 0)),
                  pl.BlockSpec((1, f, d), lambda i, e, *_: (e, 0, 0)),
                  pl.BlockSpec((1, 1, d), lambda i, e, *_: (e, 0, 0))],
        out_specs=pl.BlockSpec((MOE_TILE, d), lambda i, e, *_: (i, 0)),
        scratch_shapes=[pltpu.VMEM((N_GRANULES, MOE_GROUP * MOE_CAP, d), BF16),
                        pltpu.VMEM((N_GRANULES, MOE_GROUP * MOE_CAP, d), BF16)])
    return pl.pallas_call(
        _moe_kernel,
        grid_spec=grid_spec,
        out_shape=jax.ShapeDtypeStruct((n_tok, d), F32),
        compiler_params=pltpu.CompilerParams(
            dimension_semantics=("arbitrary", "arbitrary"),
            vmem_limit_bytes=V7X_VMEM_BYTES - 4 * 1024 * 1024),
    )(used, over, h2, slot_t, w_t, opos_t, w_up, b_up, w_down, b_down)


def _final_kernel(x_ref, y_ref, mod_ref, g_ref, b_ref, o_ref):
    gate2 = mod_ref[0, 5:6, :]
    o_ref[0] = _ln(DEEPNORM_ALPHA * x_ref[0] + gate2 * y_ref[0]) * g_ref[...] + b_ref[...]


def _final_call(x1, y, mod, g, b):
    bsz, seq, d = x1.shape
    tt = FINAL_TILE
    return pl.pallas_call(
        _final_kernel,
        grid=(bsz, seq // tt),
        in_specs=[pl.BlockSpec((1, tt, d), lambda i, j: (i, j, 0)),
                  pl.BlockSpec((1, tt, d), lambda i, j: (i, j, 0)),
                  pl.BlockSpec((1, N_MOD, d), lambda i, j: (i, 0, 0)),
                  pl.BlockSpec((1, d), lambda i, j: (0, 0)),
                  pl.BlockSpec((1, d), lambda i, j: (0, 0))],
        out_specs=pl.BlockSpec((1, tt, d), lambda i, j: (i, j, 0)),
        out_shape=jax.ShapeDtypeStruct((bsz, seq, d), F32),
    )(x1, y, mod, g, b)


def kernel(x, c, w_ada, b_ada, w_in, b_in, conv_w, conv_b, ln_a_g, ln_a_b, w_pa, b_pa, ln_v_g, ln_v_b, w_s, b_s, w_pb, b_pb, w_out, b_out, post1_g, post1_b, w_router, b_router, w_up, b_up, w_down, b_down, post2_g, post2_b):
    bsz, seq, d = x.shape
    assert w_ada.shape[0] == DEPTH == 1
    assert seq % MOE_TILE == 0 and MOE_TILE % MOE_GRANULE == 0 and MOE_GRANULE == SEQ_TILE
    assert SEQ_TILE % SUB_TILE == 0 and SUB_TILE % CHUNK == 0
    assert all(r % BF16_ROWS == 0 for r in MOE_FFN_ROWS) and MOE_FFN_ROWS[-1] == MOE_CAP
    assert list(MOE_FFN_ROWS) == sorted(set(MOE_FFN_ROWS))
    assert N_EXPERTS % MOE_GROUP == 0 and MOE_GROUP == V7X_SUBLANES
    l = 0
    row = lambda a: a.reshape(1, -1)
    mod = _ada_call(c, w_ada[l], b_ada[l]).reshape(bsz, N_MOD, d)
    bs_full = jnp.repeat(b_s[l].T, d // GMLP_HEADS, axis=1)
    x1 = _mixer_call(
        x, mod, w_in[l].astype(BF16), row(b_in[l]), conv_w[l], row(conv_b[l]),
        row(ln_a_g[l]), row(ln_a_b[l]), w_pa[l].astype(BF16), row(b_pa[l]),
        row(ln_v_g[l]), row(ln_v_b[l]), w_s[l], bs_full, w_pb[l].astype(BF16), row(b_pb[l]),
        w_out[l].astype(BF16), row(b_out[l]), row(post1_g[l]), row(post1_b[l]))
    h2, slot_t, w_t, opos_t, used, over = _route_call(x1, mod, w_router[l].T,
                                                      b_router[l].reshape(N_EXPERTS, 1))
    as_scalars = lambda a: a[:, :, 0].astype(jnp.int32).reshape(-1)
    y = _moe_call(as_scalars(used), as_scalars(over), h2.reshape(bsz * seq, d), slot_t, w_t, opos_t,
                  w_up[l].astype(BF16), b_up[l].reshape(N_EXPERTS, 1, -1),
                  w_down[l].astype(BF16), b_down[l].reshape(N_EXPERTS, 1, -1))
    return _final_call(x1, y.reshape(bsz, seq, d), mod, row(post2_g[l]), row(post2_b[l]))
```

```python
import math

import jax
import jax.numpy as jnp
from jax import lax
from jax.experimental import pallas as pl
from jax.experimental.pallas import tpu as pltpu

D_MODEL = 1024
CONV_WIDTH = 31
GMLP_HEADS = 8
CHUNK = 128
N_EXPERTS = 32
TOP_K = 4
D_EXPERT = 1024
SWIGLU_ALPHA = 1.702
SWIGLU_LIMIT = 7.0
LN_EPS = 1e-5
DEPTH = 1
DEEPNORM_ALPHA = (2.0 * DEPTH) ** 0.25
N_MOD = 6
N_PROJ = 6

V7X_VMEM_BYTES = 64 * 1024 * 1024
V7X_SUBLANES = 8
V7X_LANES = 128
BF16_ROWS = 2 * V7X_SUBLANES

SEQ_TILE = 512
SUB_TILE = 256
CONV_HALO = 32
CONV_ROWS = 64
CONV_COLS = 256
MOE_TILE = 2048
MOE_GRANULE = SEQ_TILE
MOE_CAP = 112
MOE_FFN_ROWS = (48, 64, 80, 96, MOE_CAP)
MOE_GROUP = 8
MOE_ROWS = 128
FINAL_TILE = 1024
N_GRANULES = MOE_TILE // MOE_GRANULE

F32 = jnp.float32
BF16 = jnp.bfloat16


def _ln(x):
    mu = jnp.mean(x, axis=-1, keepdims=True)
    xc = x - mu
    var = jnp.mean(xc * xc, axis=-1, keepdims=True)
    return xc * lax.rsqrt(var + LN_EPS)


def _sigmoid(x):
    return 1.0 / (1.0 + jnp.exp(-x))


def _gelu(x):
    return 0.5 * x * (1.0 + lax.erf(x * (1.0 / math.sqrt(2.0))))


def _ada_kernel(c_ref, w_ref, b_ref, o_ref):
    c = c_ref[...]
    cond = (c * _sigmoid(c)).astype(BF16)
    o_ref[...] = jnp.dot(cond, w_ref[...].astype(BF16), preferred_element_type=F32) + b_ref[...]


def _ada_call(c, w_ada, b_ada):
    bsz, d = c.shape
    n = w_ada.shape[1]
    bn = d
    return pl.pallas_call(
        _ada_kernel,
        grid=(n // bn,),
        in_specs=[pl.BlockSpec((bsz, d), lambda j: (0, 0)),
                  pl.BlockSpec((d, bn), lambda j: (0, j)),
                  pl.BlockSpec((1, bn), lambda j: (0, j))],
        out_specs=pl.BlockSpec((bsz, bn), lambda j: (0, j)),
        out_shape=jax.ShapeDtypeStruct((bsz, n), F32),
    )(c, w_ada, b_ada.reshape(1, n))


def _mixer_kernel(x_ref, mod_ref, w_in_ref, b_in_ref, cw_ref, cb_ref, lag_ref, lab_ref,
                  w_pa_ref, b_pa_ref, lvg_ref, lvb_ref, w_s_ref, bs_ref, w_pb_ref, b_pb_ref,
                  w_out_ref, b_out_ref, p1g_ref, p1b_ref,
                  x1_ref,
                  zbuf, zsh, cbuf, mbuf, pbuf):
    s = pl.program_id(1)
    ts = SEQ_TILE
    st = SUB_TILE
    d = D_MODEL

    @pl.when(s == 0)
    def _():
        zbuf[0:CONV_HALO, :] = jnp.zeros((CONV_HALO, d), F32)

    shift1 = mod_ref[0, 0:1, :]
    scale1 = mod_ref[0, 1:2, :]
    gate1 = mod_ref[0, 2:3, :]

    row = lax.broadcasted_iota(jnp.int32, (CHUNK, CHUNK), 0)
    col = lax.broadcasted_iota(jnp.int32, (CHUNK, CHUNK), 1)
    causal = col <= row
    w_mix = [jnp.where(causal, w_s_ref[hd], 0.0).astype(BF16) for hd in range(GMLP_HEADS)]
    first = CONV_HALO - (CONV_WIDTH - 1)
    sh_rows = st + CONV_HALO - V7X_SUBLANES
    hd_w = d // GMLP_HEADS

    n_chains = ts // st
    n_slabs = d // CONV_COLS
    xs_in = [x_ref[0, ci * st:(ci + 1) * st, :] for ci in range(n_chains)]
    hs = [(_ln(xv) * (1.0 + scale1) + shift1).astype(BF16) for xv in xs_in]

    def proj_slab(ci, j, n):
        lo = j * d + n * CONV_COLS
        return (jnp.dot(hs[ci], w_in_ref[:, lo:lo + CONV_COLS], preferred_element_type=F32)
                + b_in_ref[:, lo:lo + CONV_COLS])

    def glu_slab(ci, n):
        zbuf[CONV_HALO + ci * st:CONV_HALO + (ci + 1) * st, n * CONV_COLS:(n + 1) * CONV_COLS] = (
            proj_slab(ci, 0, n) * _sigmoid(proj_slab(ci, 1, n)))

    def late_slab(ci, j, n):
        pbuf[ci, j - 2, :, n * CONV_COLS:(n + 1) * CONV_COLS] = proj_slab(ci, j, n)

    def conv_block(ci, c0, r0):
        r = ci * st
        if r0 == 0:
            for b in range(1, V7X_SUBLANES):
                zsh[ci, b - 1] = zbuf[r + b:r + b + sh_rows, c0:c0 + CONV_COLS]
        acc = jnp.broadcast_to(cb_ref[:, c0:c0 + CONV_COLS], (CONV_ROWS, CONV_COLS))
        for k in range(CONV_WIDTH):
            off = first + k
            b = off % V7X_SUBLANES
            a8 = off - b + r0
            if b == 0:
                tap = zbuf[r + a8:r + a8 + CONV_ROWS, c0:c0 + CONV_COLS]
            else:
                tap = zsh[ci, b - 1, a8:a8 + CONV_ROWS, :]
            acc = acc + cw_ref[k:k + 1, c0:c0 + CONV_COLS] * tap
        cbuf[ci, r0:r0 + CONV_ROWS, c0:c0 + CONV_COLS] = acc

    for n in range(n_slabs):
        glu_slab(0, n)
    mxu_items = [(glu_slab, (ci, n)) for ci in range(1, n_chains) for n in range(n_slabs)]
    mxu_items += [(late_slab, (ci, j, n)) for ci in range(n_chains)
                  for j in range(2, N_PROJ) for n in range(n_slabs)]
    blocks = [(ci, c0, r0) for ci in range(n_chains) for c0 in range(0, d, CONV_COLS)
              for r0 in range(0, st, CONV_ROWS)]
    extra = len(mxu_items) - len(blocks)
    assert 0 <= extra <= len(blocks)
    for bi, blk in enumerate(blocks):
        conv_block(*blk)
        for _ in range(2 if bi < extra else 1):
            fn, args = mxu_items.pop(0)
            fn(*args)
    assert not mxu_items

    def chain(ci):
        r = ci * st
        x = xs_in[ci]
        a = _ln(cbuf[ci]) * lag_ref[...] + lab_ref[...]
        a = a * _sigmoid(a)
        y_a = jnp.dot(a.astype(BF16), w_pa_ref[...], preferred_element_type=F32) + b_pa_ref[...]

        u = _gelu(pbuf[ci, 0])
        v = (_ln(_gelu(pbuf[ci, 1])) * lvg_ref[...] + lvb_ref[...]).astype(BF16)
        for hd in range(GMLP_HEADS):
            for c in range(st // CHUNK):
                blk = jnp.dot(w_mix[hd], v[c * CHUNK:(c + 1) * CHUNK, hd * hd_w:(hd + 1) * hd_w],
                              preferred_element_type=F32)
                mbuf[ci, c * CHUNK:(c + 1) * CHUNK, hd * hd_w:(hd + 1) * hd_w] = (
                    blk + bs_ref[:, hd * hd_w:(hd + 1) * hd_w])
        gated = (u * mbuf[ci]).astype(BF16)
        y_b = jnp.dot(gated, w_pb_ref[...], preferred_element_type=F32) + b_pb_ref[...]

        merged = (_sigmoid(pbuf[ci, 2]) * y_a + _sigmoid(pbuf[ci, 3]) * y_b).astype(BF16)
        y = jnp.dot(merged, w_out_ref[...], preferred_element_type=F32) + b_out_ref[...]

        x1 = _ln(DEEPNORM_ALPHA * x + gate1 * y) * p1g_ref[...] + p1b_ref[...]
        x1_ref[0, r:r + st, :] = x1

    for ci in range(ts // st):
        chain(ci)
    zbuf[0:CONV_HALO, :] = zbuf[ts:ts + CONV_HALO, :]


def _const_spec(shape):
    nd = len(shape)
    return pl.BlockSpec(shape, lambda b, s: (0,) * nd, pipeline_mode=pl.Buffered(1))


def _mixer_call(x, mod, w_in, b_in, conv_w, conv_b, ln_a_g, ln_a_b, w_pa, b_pa, ln_v_g, ln_v_b,
                w_s, bs_full, w_pb, b_pb, w_out, b_out, post1_g, post1_b):
    bsz, seq, d = x.shape
    ts = SEQ_TILE
    n_chains = ts // SUB_TILE
    consts = [w_in, b_in, conv_w, conv_b, ln_a_g, ln_a_b, w_pa, b_pa, ln_v_g, ln_v_b,
              w_s, bs_full, w_pb, b_pb, w_out, b_out, post1_g, post1_b]
    in_specs = [pl.BlockSpec((1, ts, d), lambda b, s: (b, s, 0)),
                pl.BlockSpec((1, N_MOD, d), lambda b, s: (b, 0, 0))]
    in_specs += [_const_spec(a.shape) for a in consts]
    sh_rows = SUB_TILE + CONV_HALO - V7X_SUBLANES
    return pl.pallas_call(
        _mixer_kernel,
        grid=(bsz, seq // ts),
        in_specs=in_specs,
        out_specs=pl.BlockSpec((1, ts, d), lambda b, s: (b, s, 0)),
        out_shape=jax.ShapeDtypeStruct((bsz, seq, d), F32),
        scratch_shapes=[pltpu.VMEM((CONV_HALO + ts, d), F32),
                        pltpu.VMEM((n_chains, V7X_SUBLANES - 1, sh_rows, CONV_COLS), F32),
                        pltpu.VMEM((n_chains, SUB_TILE, d), F32),
                        pltpu.VMEM((n_chains, SUB_TILE, d), F32),
                        pltpu.VMEM((n_chains, N_PROJ - 2, SUB_TILE, d), F32)],
        compiler_params=pltpu.CompilerParams(
            dimension_semantics=("arbitrary", "arbitrary"),
            vmem_limit_bytes=V7X_VMEM_BYTES - 8 * 1024 * 1024),
    )(x, mod, *consts)


def _route_kernel(x1_ref, mod_ref, w_rt_ref, b_r_ref,
                  h2_ref, slot_ref, wt_ref, opos_ref, used_ref, over_ref, lbuf):
    gr = MOE_GRANULE
    shift2 = mod_ref[0, 3:4, :]
    scale2 = mod_ref[0, 4:5, :]
    for g in range(N_GRANULES):
        h2 = _ln(x1_ref[0, g * gr:(g + 1) * gr, :]) * (1.0 + scale2) + shift2
        h2_ref[0, g * gr:(g + 1) * gr, :] = h2.astype(BF16)
        lbuf[:, g * gr:(g + 1) * gr] = lax.dot_general(
            w_rt_ref[...], h2, (((1,), (1,)), ((), ())), precision=lax.Precision.HIGHEST,
            preferred_element_type=F32) + b_r_ref[...]

    logits = lbuf[...]
    e_iota = lax.broadcasted_iota(jnp.int32, (N_EXPERTS, MOE_TILE), 0)
    work = logits
    sel = jnp.zeros((N_EXPERTS, MOE_TILE), F32)
    top1 = None
    for k in range(TOP_K):
        m = jnp.max(work, axis=0, keepdims=True)
        idx = jnp.min(jnp.where(work == m, e_iota, N_EXPERTS), axis=0, keepdims=True)
        pick = e_iota == idx
        if k == 0:
            top1 = m
        sel = jnp.where(pick, 1.0, sel)
        work = jnp.where(pick, -jnp.inf, work)
    chosen = sel > 0.0
    ex = jnp.where(chosen, jnp.exp(logits - top1), 0.0)
    wt_ref[...] = ex / jnp.sum(ex, axis=0, keepdims=True)

    t_row = lax.broadcasted_iota(jnp.int32, (gr, gr), 0)
    t_col = lax.broadcasted_iota(jnp.int32, (gr, gr), 1)
    before = jnp.where(t_row < t_col, 1.0, 0.0).astype(BF16)
    obase = jnp.zeros((N_EXPERTS, 1), F32)
    used = jnp.zeros((N_EXPERTS, 1), F32)
    for g in range(N_GRANULES):
        cols = slice(g * gr, (g + 1) * gr)
        sel_g = sel[:, cols]
        rank = jnp.dot(sel_g.astype(BF16), before, preferred_element_type=F32)
        picked = sel_g > 0.0
        fits = jnp.logical_and(picked, rank < float(MOE_CAP))
        slot = jnp.where(fits, rank, -1.0)
        slot_ref[:, cols] = slot
        used = jnp.maximum(used, jnp.max(slot, axis=1, keepdims=True) + 1.0)
        over = jnp.where(jnp.logical_and(picked, rank >= float(MOE_CAP)), 1.0, 0.0)
        orank = jnp.dot(over.astype(BF16), before, preferred_element_type=F32) + obase
        opos_ref[:, cols] = jnp.where(over > 0.0, orank, -1.0)
        obase = obase + jnp.sum(over, axis=1, keepdims=True)
    used_ref[0] = jnp.broadcast_to(used, (N_EXPERTS, V7X_LANES))
    over_ref[0] = jnp.broadcast_to(obase, (N_EXPERTS, V7X_LANES))


def _route_call(x1, mod, w_rt, b_r):
    bsz, seq, d = x1.shape
    n_t = seq // MOE_TILE
    route_spec = pl.BlockSpec((N_EXPERTS, MOE_TILE), lambda b, t: (0, b * n_t + t))
    route_shape = jax.ShapeDtypeStruct((N_EXPERTS, bsz * seq), F32)
    tile_spec = pl.BlockSpec((1, N_EXPERTS, V7X_LANES), lambda b, t: (b * n_t + t, 0, 0))
    tile_shape = jax.ShapeDtypeStruct((bsz * n_t, N_EXPERTS, V7X_LANES), F32)
    return pl.pallas_call(
        _route_kernel,
        grid=(bsz, n_t),
        in_specs=[pl.BlockSpec((1, MOE_TILE, d), lambda b, t: (b, t, 0)),
                  pl.BlockSpec((1, N_MOD, d), lambda b, t: (b, 0, 0)),
                  pl.BlockSpec(w_rt.shape, lambda b, t: (0, 0)),
                  pl.BlockSpec(b_r.shape, lambda b, t: (0, 0))],
        out_specs=[pl.BlockSpec((1, MOE_TILE, d), lambda b, t: (b, t, 0)),
                   route_spec, route_spec, route_spec, tile_spec, tile_spec],
        out_shape=[jax.ShapeDtypeStruct((bsz, seq, d), BF16),
                   route_shape, route_shape, route_shape, tile_shape, tile_shape],
        scratch_shapes=[pltpu.VMEM((N_EXPERTS, MOE_TILE), F32)],
        compiler_params=pltpu.CompilerParams(
            dimension_semantics=("arbitrary", "arbitrary"),
            vmem_limit_bytes=V7X_VMEM_BYTES - 16 * 1024 * 1024),
    )(x1, mod, w_rt, b_r)


def _expert(xg, w_up_ref, b_up_ref, w_dn_ref, b_dn_ref):
    f = D_EXPERT
    zz = jnp.dot(xg, w_up_ref[0], preferred_element_type=F32) + b_up_ref[0]
    glu = jnp.minimum(zz[:, :f], SWIGLU_LIMIT)
    lin = jnp.clip(zz[:, f:], -SWIGLU_LIMIT, SWIGLU_LIMIT)
    act = (glu * _sigmoid(SWIGLU_ALPHA * glu) * (lin + 1.0)).astype(BF16)
    return (jnp.dot(act, w_dn_ref[0], preferred_element_type=F32) + b_dn_ref[0]).astype(BF16)


def _moe_kernel(used_ref, over_ref, h_ref, slot_ref, wt_ref, opos_ref,
                w_up_ref, b_up_ref, w_dn_ref, b_dn_ref, o_ref, xs, ys):
    e = pl.program_id(1)
    member = e % MOE_GROUP
    base = pl.multiple_of(e - member, MOE_GROUP)
    gr = MOE_GRANULE

    @pl.when(e == 0)
    def _():
        o_ref[...] = jnp.zeros_like(o_ref)

    def onehot(g, weighted):
        slots = slot_ref[pl.ds(base, MOE_GROUP), g * gr:(g + 1) * gr]
        if weighted:
            wts = wt_ref[pl.ds(base, MOE_GROUP), g * gr:(g + 1) * gr]
        cap = lax.broadcasted_iota(jnp.int32, (MOE_CAP, 1), 0).astype(F32)
        blocks = []
        for m in range(MOE_GROUP):
            hit = slots[m:m + 1, :] == cap
            val = wts[m:m + 1, :] if weighted else 1.0
            blocks.append(jnp.where(hit, val, 0.0).astype(BF16))
        return jnp.concatenate(blocks, axis=0)

    @pl.when(member == 0)
    def _():
        for g in range(N_GRANULES):
            xs[g] = jnp.dot(onehot(g, False), h_ref[g * gr:(g + 1) * gr, :],
                            preferred_element_type=F32).astype(BF16)

    r0 = pl.multiple_of(member * MOE_CAP, BF16_ROWS)
    used = used_ref[pl.program_id(0) * N_EXPERTS + e]

    def run_expert(rows):
        xe = jnp.concatenate([xs[g, pl.ds(r0, rows), :] for g in range(N_GRANULES)], axis=0)
        ye = _expert(xe, w_up_ref, b_up_ref, w_dn_ref, b_dn_ref)
        for g in range(N_GRANULES):
            ys[g, pl.ds(r0, rows), :] = ye[g * rows:(g + 1) * rows, :]
            if rows < MOE_CAP:
                rest = pl.multiple_of(r0 + rows, BF16_ROWS)
                ys[g, pl.ds(rest, MOE_CAP - rows), :] = jnp.zeros((MOE_CAP - rows, D_MODEL), BF16)

    for below, rows in zip((-1,) + MOE_FFN_ROWS[:-1], MOE_FFN_ROWS):
        pl.when(jnp.logical_and(used > below, used <= rows))(lambda rows=rows: run_expert(rows))

    @pl.when(member == MOE_GROUP - 1)
    def _():
        for g in range(N_GRANULES):
            o_ref[g * gr:(g + 1) * gr, :] += lax.dot_general(
                onehot(g, True), ys[g], (((0,), (0,)), ((), ())), preferred_element_type=F32)

    pos_row = opos_ref[pl.ds(e, 1), :]
    w_row = wt_ref[pl.ds(e, 1), :]
    count = over_ref[pl.program_id(0) * N_EXPERTS + e]
    n_sub = (count + (MOE_ROWS - 1)) // MOE_ROWS

    def body(sb, carry):
        slot = (lax.broadcasted_iota(jnp.int32, (MOE_ROWS, 1), 0) + sb * MOE_ROWS).astype(F32)
        hit = pos_row == slot
        gather = jnp.where(hit, 1.0, 0.0).astype(BF16)
        xg = jnp.dot(gather, h_ref[...], preferred_element_type=F32).astype(BF16)
        y = _expert(xg, w_up_ref, b_up_ref, w_dn_ref, b_dn_ref)
        scatter = jnp.where(hit, w_row, 0.0).astype(BF16)
        o_ref[...] += lax.dot_general(scatter, y, (((0,), (0,)), ((), ())),
                                      preferred_element_type=F32)
        return carry

    lax.fori_loop(0, n_sub, body, 0)


def _moe_call(used, over, h2, slot_t, w_t, opos_t, w_up, b_up, w_down, b_down):
    n_tok, d = h2.shape
    n_tiles = n_tok // MOE_TILE
    f2 = w_up.shape[2]
    f = w_down.shape[1]
    route_spec = pl.BlockSpec((N_EXPERTS, MOE_TILE), lambda i, e, *_: (0, i))
    grid_spec = pltpu.PrefetchScalarGridSpec(
        num_scalar_prefetch=2,
        grid=(n_tiles, N_EXPERTS),
        in_specs=[pl.BlockSpec((MOE_TILE, d), lambda i, e, *_: (i, 0),
                               pipeline_mode=pl.Buffered(1)),
                  route_spec, route_spec, route_spec,
                  pl.BlockSpec((1, d, f2), lambda i, e, *_: (e, 0, 0)),
                  pl.BlockSpec((1, 1, f2), lambda i, e, *_: (e, 0, 0)),
                  pl.BlockSpec((1, f, d), lambda i, e, *_: (e, 0, 0)),
                  pl.BlockSpec((1, 1, d), lambda i, e, *_: (e, 0, 0))],
        out_specs=pl.BlockSpec((MOE_TILE, d), lambda i, e, *_: (i, 0)),
        scratch_shapes=[pltpu.VMEM((N_GRANULES, MOE_GROUP * MOE_CAP, d), BF16),
                        pltpu.VMEM((N_GRANULES, MOE_GROUP * MOE_CAP, d), BF16)])
    return pl.pallas_call(
        _moe_kernel,
        grid_spec=grid_spec,
        out_shape=jax.ShapeDtypeStruct((n_tok, d), F32),
        compiler_params=pltpu.CompilerParams(
            dimension_semantics=("arbitrary", "arbitrary"),
            vmem_limit_bytes=V7X_VMEM_BYTES - 4 * 1024 * 1024),
    )(used, over, h2, slot_t, w_t, opos_t, w_up, b_up, w_down, b_down)


def _final_kernel(x_ref, y_ref, mod_ref, g_ref, b_ref, o_ref):
    gate2 = mod_ref[0, 5:6, :]
    o_ref[0] = _ln(DEEPNORM_ALPHA * x_ref[0] + gate2 * y_ref[0]) * g_ref[...] + b_ref[...]


def _final_call(x1, y, mod, g, b):
    bsz, seq, d = x1.shape
    tt = FINAL_TILE
    return pl.pallas_call(
        _final_kernel,
        grid=(bsz, seq // tt),
        in_specs=[pl.BlockSpec((1, tt, d), lambda i, j: (i, j, 0)),
                  pl.BlockSpec((1, tt, d), lambda i, j: (i, j, 0)),
                  pl.BlockSpec((1, N_MOD, d), lambda i, j: (i, 0, 0)),
                  pl.BlockSpec((1, d), lambda i, j: (0, 0)),
                  pl.BlockSpec((1, d), lambda i, j: (0, 0))],
        out_specs=pl.BlockSpec((1, tt, d), lambda i, j: (i, j, 0)),
        out_shape=jax.ShapeDtypeStruct((bsz, seq, d), F32),
    )(x1, y, mod, g, b)


def kernel(x, c, w_ada, b_ada, w_in, b_in, conv_w, conv_b, ln_a_g, ln_a_b, w_pa, b_pa, ln_v_g, ln_v_b, w_s, b_s, w_pb, b_pb, w_out, b_out, post1_g, post1_b, w_router, b_router, w_up, b_up, w_down, b_down, post2_g, post2_b):
    bsz, seq, d = x.shape
    assert w_ada.shape[0] == DEPTH == 1
    assert seq % MOE_TILE == 0 and MOE_TILE % MOE_GRANULE == 0 and MOE_GRANULE == SEQ_TILE
    assert SEQ_TILE % SUB_TILE == 0 and SUB_TILE % CHUNK == 0
    assert all(r % BF16_ROWS == 0 for r in MOE_FFN_ROWS) and MOE_FFN_ROWS[-1] == MOE_CAP
    assert list(MOE_FFN_ROWS) == sorted(set(MOE_FFN_ROWS))
    assert N_EXPERTS % MOE_GROUP == 0 and MOE_GROUP == V7X_SUBLANES
    l = 0
    row = lambda a: a.reshape(1, -1)
    mod = _ada_call(c, w_ada[l], b_ada[l]).reshape(bsz, N_MOD, d)
    bs_full = jnp.repeat(b_s[l].T, d // GMLP_HEADS, axis=1)
    x1 = _mixer_call(
        x, mod, w_in[l].astype(BF16), row(b_in[l]), conv_w[l], row(conv_b[l]),
        row(ln_a_g[l]), row(ln_a_b[l]), w_pa[l].astype(BF16), row(b_pa[l]),
        row(ln_v_g[l]), row(ln_v_b[l]), w_s[l], bs_full, w_pb[l].astype(BF16), row(b_pb[l]),
        w_out[l].astype(BF16), row(b_out[l]), row(post1_g[l]), row(post1_b[l]))
    h2, slot_t, w_t, opos_t, used, over = _route_call(x1, mod, w_router[l].T,
                                                      b_router[l].reshape(N_EXPERTS, 1))
    as_scalars = lambda a: a[:, :, 0].astype(jnp.int32).reshape(-1)
    y = _moe_call(as_scalars(used), as_scalars(over), h2.reshape(bsz * seq, d), slot_t, w_t, opos_t,
                  w_up[l].astype(BF16), b_up[l].reshape(N_EXPERTS, 1, -1),
                  w_down[l].astype(BF16), b_down[l].reshape(N_EXPERTS, 1, -1))
    return _final_call(x1, y.reshape(bsz, seq, d), mod, row(post2_g[l]), row(post2_b[l]))
```

```python
import math

import jax
import jax.numpy as jnp
from jax import lax
from jax.experimental import pallas as pl
from jax.experimental.pallas import tpu as pltpu

D_MODEL = 1024
CONV_WIDTH = 31
GMLP_HEADS = 8
CHUNK = 128
N_EXPERTS = 32
TOP_K = 4
D_EXPERT = 1024
SWIGLU_ALPHA = 1.702
SWIGLU_LIMIT = 7.0
LN_EPS = 1e-5
DEPTH = 1
DEEPNORM_ALPHA = (2.0 * DEPTH) ** 0.25
N_MOD = 6
N_PROJ = 6

V7X_VMEM_BYTES = 64 * 1024 * 1024
V7X_SUBLANES = 8
V7X_LANES = 128
BF16_ROWS = 2 * V7X_SUBLANES

SEQ_TILE = 512
SUB_TILE = 256
CONV_HALO = 32
CONV_ROWS = 64
CONV_COLS = 256
MOE_TILE = 2048
MOE_GRANULE = SEQ_TILE
MOE_CAP = 112
MOE_FFN_ROWS = (48, 64, 80, 96, MOE_CAP)
MOE_GROUP = 8
MOE_ROWS = 128
FINAL_TILE = 1024
N_GRANULES = MOE_TILE // MOE_GRANULE

F32 = jnp.float32
BF16 = jnp.bfloat16


def _ln(x):
    mu = jnp.mean(x, axis=-1, keepdims=True)
    xc = x - mu
    var = jnp.mean(xc * xc, axis=-1, keepdims=True)
    return xc * lax.rsqrt(var + LN_EPS)


def _sigmoid(x):
    return 1.0 / (1.0 + jnp.exp(-x))


def _gelu(x):
    return 0.5 * x * (1.0 + lax.erf(x * (1.0 / math.sqrt(2.0))))


def _ada_kernel(c_ref, w_ref, b_ref, o_ref):
    c = c_ref[...]
    cond = (c * _sigmoid(c)).astype(BF16)
    o_ref[...] = jnp.dot(cond, w_ref[...].astype(BF16), preferred_element_type=F32) + b_ref[...]


def _ada_call(c, w_ada, b_ada):
    bsz, d = c.shape
    n = w_ada.shape[1]
    bn = d
    return pl.pallas_call(
        _ada_kernel,
        grid=(n // bn,),
        in_specs=[pl.BlockSpec((bsz, d), lambda j: (0, 0)),
                  pl.BlockSpec((d, bn), lambda j: (0, j)),
                  pl.BlockSpec((1, bn), lambda j: (0, j))],
        out_specs=pl.BlockSpec((bsz, bn), lambda j: (0, j)),
        out_shape=jax.ShapeDtypeStruct((bsz, n), F32),
    )(c, w_ada, b_ada.reshape(1, n))


def _mixer_kernel(x_ref, mod_ref, w_in_ref, b_in_ref, cw_ref, cb_ref, lag_ref, lab_ref,
                  w_pa_ref, b_pa_ref, lvg_ref, lvb_ref, w_s_ref, bs_ref, w_pb_ref, b_pb_ref,
                  w_out_ref, b_out_ref, p1g_ref, p1b_ref,
                  x1_ref,
                  zbuf, zsh, cbuf, mbuf, pbuf):
    s = pl.program_id(1)
    ts = SEQ_TILE
    st = SUB_TILE
    d = D_MODEL

    @pl.when(s == 0)
    def _():
        zbuf[0:CONV_HALO, :] = jnp.zeros((CONV_HALO, d), F32)

    shift1 = mod_ref[0, 0:1, :]
    scale1 = mod_ref[0, 1:2, :]
    gate1 = mod_ref[0, 2:3, :]

    row = lax.broadcasted_iota(jnp.int32, (CHUNK, CHUNK), 0)
    col = lax.broadcasted_iota(jnp.int32, (CHUNK, CHUNK), 1)
    causal = col <= row
    w_mix = [jnp.where(causal, w_s_ref[hd], 0.0).astype(BF16) for hd in range(GMLP_HEADS)]
    first = CONV_HALO - (CONV_WIDTH - 1)
    sh_rows = st + CONV_HALO - V7X_SUBLANES
    hd_w = d // GMLP_HEADS

    n_chains = ts // st
    n_slabs = d // CONV_COLS
    xs_in = [x_ref[0, ci * st:(ci + 1) * st, :] for ci in range(n_chains)]
    hs = [(_ln(xv) * (1.0 + scale1) + shift1).astype(BF16) for xv in xs_in]

    def proj_slab(ci, j, n):
        lo = j * d + n * CONV_COLS
        return (jnp.dot(hs[ci], w_in_ref[:, lo:lo + CONV_COLS], preferred_element_type=F32)
                + b_in_ref[:, lo:lo + CONV_COLS])

    def glu_slab(ci, n):
        zbuf[CONV_HALO + ci * st:CONV_HALO + (ci + 1) * st, n * CONV_COLS:(n + 1) * CONV_COLS] = (
            proj_slab(ci, 0, n) * _sigmoid(proj_slab(ci, 1, n)))

    def late_slab(ci, j, n):
        pbuf[ci, j - 2, :, n * CONV_COLS:(n + 1) * CONV_COLS] = proj_slab(ci, j, n)

    def conv_block(ci, c0, r0):
        r = ci * st
        if r0 == 0:
            for b in range(1, V7X_SUBLANES):
                zsh[ci, b - 1] = zbuf[r + b:r + b + sh_rows, c0:c0 + CONV_COLS]
        acc = jnp.broadcast_to(cb_ref[:, c0:c0 + CONV_COLS], (CONV_ROWS, CONV_COLS))
        for k in range(CONV_WIDTH):
            off = first + k
            b = off % V7X_SUBLANES
            a8 = off - b + r0
            if b == 0:
                tap = zbuf[r + a8:r + a8 + CONV_ROWS, c0:c0 + CONV_COLS]
            else:
                tap = zsh[ci, b - 1, a8:a8 + CONV_ROWS, :]
            acc = acc + cw_ref[k:k + 1, c0:c0 + CONV_COLS] * tap
        cbuf[ci, r0:r0 + CONV_ROWS, c0:c0 + CONV_COLS] = acc

    for n in range(n_slabs):
        glu_slab(0, n)
    mxu_items = [(glu_slab, (ci, n)) for ci in range(1, n_chains) for n in range(n_slabs)]
    mxu_items += [(late_slab, (ci, j, n)) for ci in range(n_chains)
                  for j in range(2, N_PROJ) for n in range(n_slabs)]
    blocks = [(ci, c0, r0) for ci in range(n_chains) for c0 in range(0, d, CONV_COLS)
              for r0 in range(0, st, CONV_ROWS)]
    extra = len(mxu_items) - len(blocks)
    assert 0 <= extra <= len(blocks)
    for bi, blk in enumerate(blocks):
        conv_block(*blk)
        for _ in range(2 if bi < extra else 1):
            fn, args = mxu_items.pop(0)
            fn(*args)
    assert not mxu_items

    def chain(ci):
        r = ci * st
        x = xs_in[ci]
        a = _ln(cbuf[ci]) * lag_ref[...] + lab_ref[...]
        a = a * _sigmoid(a)
        y_a = jnp.dot(a.astype(BF16), w_pa_ref[...], preferred_element_type=F32) + b_pa_ref[...]

        u = _gelu(pbuf[ci, 0])
        v = (_ln(_gelu(pbuf[ci, 1])) * lvg_ref[...] + lvb_ref[...]).astype(BF16)
        for hd in range(GMLP_HEADS):
            for c in range(st // CHUNK):
                blk = jnp.dot(w_mix[hd], v[c * CHUNK:(c + 1) * CHUNK, hd * hd_w:(hd + 1) * hd_w],
                              preferred_element_type=F32)
                mbuf[ci, c * CHUNK:(c + 1) * CHUNK, hd * hd_w:(hd + 1) * hd_w] = (
                    blk + bs_ref[:, hd * hd_w:(hd + 1) * hd_w])
        gated = (u * mbuf[ci]).astype(BF16)
        y_b = jnp.dot(gated, w_pb_ref[...], preferred_element_type=F32) + b_pb_ref[...]

        merged = (_sigmoid(pbuf[ci, 2]) * y_a + _sigmoid(pbuf[ci, 3]) * y_b).astype(BF16)
        y = jnp.dot(merged, w_out_ref[...], preferred_element_type=F32) + b_out_ref[...]

        x1 = _ln(DEEPNORM_ALPHA * x + gate1 * y) * p1g_ref[...] + p1b_ref[...]
        x1_ref[0, r:r + st, :] = x1

    for ci in range(ts // st):
        chain(ci)
    zbuf[0:CONV_HALO, :] = zbuf[ts:ts + CONV_HALO, :]


def _const_spec(shape):
    nd = len(shape)
    return pl.BlockSpec(shape, lambda b, s: (0,) * nd, pipeline_mode=pl.Buffered(1))


def _mixer_call(x, mod, w_in, b_in, conv_w, conv_b, ln_a_g, ln_a_b, w_pa, b_pa, ln_v_g, ln_v_b,
                w_s, bs_full, w_pb, b_pb, w_out, b_out, post1_g, post1_b):
    bsz, seq, d = x.shape
    ts = SEQ_TILE
    n_chains = ts // SUB_TILE
    consts = [w_in, b_in, conv_w, conv_b, ln_a_g, ln_a_b, w_pa, b_pa, ln_v_g, ln_v_b,
              w_s, bs_full, w_pb, b_pb, w_out, b_out, post1_g, post1_b]
    in_specs = [pl.BlockSpec((1, ts, d), lambda b, s: (b, s, 0)),
                pl.BlockSpec((1, N_MOD, d), lambda b, s: (b, 0, 0))]
    in_specs += [_const_spec(a.shape) for a in consts]
    sh_rows = SUB_TILE + CONV_HALO - V7X_SUBLANES
    return pl.pallas_call(
        _mixer_kernel,
        grid=(bsz, seq // ts),
        in_specs=in_specs,
        out_specs=pl.BlockSpec((1, ts, d), lambda b, s: (b, s, 0)),
        out_shape=jax.ShapeDtypeStruct((bsz, seq, d), F32),
        scratch_shapes=[pltpu.VMEM((CONV_HALO + ts, d), F32),
                        pltpu.VMEM((n_chains, V7X_SUBLANES - 1, sh_rows, CONV_COLS), F32),
                        pltpu.VMEM((n_chains, SUB_TILE, d), F32),
                        pltpu.VMEM((n_chains, SUB_TILE, d), F32),
                        pltpu.VMEM((n_chains, N_PROJ - 2, SUB_TILE, d), F32)],
        compiler_params=pltpu.CompilerParams(
            dimension_semantics=("arbitrary", "arbitrary"),
            vmem_limit_bytes=V7X_VMEM_BYTES - 8 * 1024 * 1024),
    )(x, mod, *consts)


def _route_kernel(x1_ref, mod_ref, w_rt_ref, b_r_ref,
                  h2_ref, slot_ref, wt_ref, opos_ref, used_ref, over_ref, lbuf):
    gr = MOE_GRANULE
    shift2 = mod_ref[0, 3:4, :]
    scale2 = mod_ref[0, 4:5, :]
    nt = (((1,), (1,)), ((), ()))
    w_rt = w_rt_ref[...]
    w_hi = w_rt.astype(BF16)
    w_lo = (w_rt - w_hi.astype(F32)).astype(BF16)
    for g in range(N_GRANULES):
        h2 = _ln(x1_ref[0, g * gr:(g + 1) * gr, :]) * (1.0 + scale2) + shift2
        h_hi = h2.astype(BF16)
        h2_ref[0, g * gr:(g + 1) * gr, :] = h_hi
        h_lo = (h2 - h_hi.astype(F32)).astype(BF16)
        lbuf[:, g * gr:(g + 1) * gr] = (
            lax.dot_general(w_hi, h_hi, nt, preferred_element_type=F32)
            + lax.dot_general(w_lo, h_hi, nt, preferred_element_type=F32)
            + lax.dot_general(w_hi, h_lo, nt, preferred_element_type=F32)
            + b_r_ref[...])

    logits = lbuf[...]
    e_iota = lax.broadcasted_iota(jnp.int32, (N_EXPERTS, MOE_TILE), 0)
    work = logits
    sel = jnp.zeros((N_EXPERTS, MOE_TILE), F32)
    top1 = None
    for k in range(TOP_K):
        m = jnp.max(work, axis=0, keepdims=True)
        idx = jnp.min(jnp.where(work == m, e_iota, N_EXPERTS), axis=0, keepdims=True)
        pick = e_iota == idx
        if k == 0:
            top1 = m
        sel = jnp.where(pick, 1.0, sel)
        work = jnp.where(pick, -jnp.inf, work)
    chosen = sel > 0.0
    ex = jnp.where(chosen, jnp.exp(logits - top1), 0.0)
    wt_ref[...] = ex / jnp.sum(ex, axis=0, keepdims=True)

    t_row = lax.broadcasted_iota(jnp.int32, (gr, gr), 0)
    t_col = lax.broadcasted_iota(jnp.int32, (gr, gr), 1)
    before = jnp.where(t_row < t_col, 1.0, 0.0).astype(BF16)
    obase = jnp.zeros((N_EXPERTS, 1), F32)
    used = jnp.zeros((N_EXPERTS, 1), F32)
    for g in range(N_GRANULES):
        cols = slice(g * gr, (g + 1) * gr)
        sel_g = sel[:, cols]
        rank = jnp.dot(sel_g.astype(BF16), before, preferred_element_type=F32)
        picked = sel_g > 0.0
        fits = jnp.logical_and(picked, rank < float(MOE_CAP))
        slot = jnp.where(fits, rank, -1.0)
        slot_ref[:, cols] = slot
        used = jnp.maximum(used, jnp.max(slot, axis=1, keepdims=True) + 1.0)
        over = jnp.where(jnp.logical_and(picked, rank >= float(MOE_CAP)), 1.0, 0.0)
        orank = jnp.dot(over.astype(BF16), before, preferred_element_type=F32) + obase
        opos_ref[:, cols] = jnp.where(over > 0.0, orank, -1.0)
        obase = obase + jnp.sum(over, axis=1, keepdims=True)
    used_ref[0] = jnp.broadcast_to(used, (N_EXPERTS, V7X_LANES))
    over_ref[0] = jnp.broadcast_to(obase, (N_EXPERTS, V7X_LANES))


def _route_call(x1, mod, w_rt, b_r):
    bsz, seq, d = x1.shape
    n_t = seq // MOE_TILE
    route_spec = pl.BlockSpec((N_EXPERTS, MOE_TILE), lambda b, t: (0, b * n_t + t))
    route_shape = jax.ShapeDtypeStruct((N_EXPERTS, bsz * seq), F32)
    tile_spec = pl.BlockSpec((1, N_EXPERTS, V7X_LANES), lambda b, t: (b * n_t + t, 0, 0))
    tile_shape = jax.ShapeDtypeStruct((bsz * n_t, N_EXPERTS, V7X_LANES), F32)
    return pl.pallas_call(
        _route_kernel,
        grid=(bsz, n_t),
        in_specs=[pl.BlockSpec((1, MOE_TILE, d), lambda b, t: (b, t, 0)),
                  pl.BlockSpec((1, N_MOD, d), lambda b, t: (b, 0, 0)),
                  pl.BlockSpec(w_rt.shape, lambda b, t: (0, 0)),
                  pl.BlockSpec(b_r.shape, lambda b, t: (0, 0))],
        out_specs=[pl.BlockSpec((1, MOE_TILE, d), lambda b, t: (b, t, 0)),
                   route_spec, route_spec, route_spec, tile_spec, tile_spec],
        out_shape=[jax.ShapeDtypeStruct((bsz, seq, d), BF16),
                   route_shape, route_shape, route_shape, tile_shape, tile_shape],
        scratch_shapes=[pltpu.VMEM((N_EXPERTS, MOE_TILE), F32)],
        compiler_params=pltpu.CompilerParams(
            dimension_semantics=("arbitrary", "arbitrary"),
            vmem_limit_bytes=V7X_VMEM_BYTES - 16 * 1024 * 1024),
    )(x1, mod, w_rt, b_r)


def _expert(xg, w_up_ref, b_up_ref, w_dn_ref, b_dn_ref):
    f = D_EXPERT
    zz = jnp.dot(xg, w_up_ref[0], preferred_element_type=F32) + b_up_ref[0]
    glu = jnp.minimum(zz[:, :f], SWIGLU_LIMIT)
    lin = jnp.clip(zz[:, f:], -SWIGLU_LIMIT, SWIGLU_LIMIT)
    act = (glu * _sigmoid(SWIGLU_ALPHA * glu) * (lin + 1.0)).astype(BF16)
    return (jnp.dot(act, w_dn_ref[0], preferred_element_type=F32) + b_dn_ref[0]).astype(BF16)


def _moe_kernel(used_ref, over_ref, h_ref, slot_ref, wt_ref, opos_ref,
                w_up_ref, b_up_ref, w_dn_ref, b_dn_ref, o_ref, xs, ys):
    e = pl.program_id(1)
    member = e % MOE_GROUP
    base = pl.multiple_of(e - member, MOE_GROUP)
    gr = MOE_GRANULE

    @pl.when(e == 0)
    def _():
        o_ref[...] = jnp.zeros_like(o_ref)

    def onehot(g, weighted):
        slots = slot_ref[pl.ds(base, MOE_GROUP), g * gr:(g + 1) * gr]
        if weighted:
            wts = wt_ref[pl.ds(base, MOE_GROUP), g * gr:(g + 1) * gr]
        cap = lax.broadcasted_iota(jnp.int32, (MOE_CAP, 1), 0).astype(F32)
        blocks = []
        for m in range(MOE_GROUP):
            hit = slots[m:m + 1, :] == cap
            val = wts[m:m + 1, :] if weighted else 1.0
            blocks.append(jnp.where(hit, val, 0.0).astype(BF16))
        return jnp.concatenate(blocks, axis=0)

    @pl.when(member == 0)
    def _():
        for g in range(N_GRANULES):
            xs[g] = jnp.dot(onehot(g, False), h_ref[g * gr:(g + 1) * gr, :],
                            preferred_element_type=F32).astype(BF16)

    r0 = pl.multiple_of(member * MOE_CAP, BF16_ROWS)
    used = used_ref[pl.program_id(0) * N_EXPERTS + e]

    def run_expert(rows):
        xe = jnp.concatenate([xs[g, pl.ds(r0, rows), :] for g in range(N_GRANULES)], axis=0)
        ye = _expert(xe, w_up_ref, b_up_ref, w_dn_ref, b_dn_ref)
        for g in range(N_GRANULES):
            ys[g, pl.ds(r0, rows), :] = ye[g * rows:(g + 1) * rows, :]
            if rows < MOE_CAP:
                rest = pl.multiple_of(r0 + rows, BF16_ROWS)
                ys[g, pl.ds(rest, MOE_CAP - rows), :] = jnp.zeros((MOE_CAP - rows, D_MODEL), BF16)

    for below, rows in zip((-1,) + MOE_FFN_ROWS[:-1], MOE_FFN_ROWS):
        pl.when(jnp.logical_and(used > below, used <= rows))(lambda rows=rows: run_expert(rows))

    @pl.when(member == MOE_GROUP - 1)
    def _():
        for g in range(N_GRANULES):
            o_ref[g * gr:(g + 1) * gr, :] += lax.dot_general(
                onehot(g, True), ys[g], (((0,), (0,)), ((), ())), preferred_element_type=F32)

    pos_row = opos_ref[pl.ds(e, 1), :]
    w_row = wt_ref[pl.ds(e, 1), :]
    count = over_ref[pl.program_id(0) * N_EXPERTS + e]
    n_sub = (count + (MOE_ROWS - 1)) // MOE_ROWS

    def body(sb, carry):
        slot = (lax.broadcasted_iota(jnp.int32, (MOE_ROWS, 1), 0) + sb * MOE_ROWS).astype(F32)
        hit = pos_row == slot
        gather = jnp.where(hit, 1.0, 0.0).astype(BF16)
        xg = jnp.dot(gather, h_ref[...], preferred_element_type=F32).astype(BF16)
        y = _expert(xg, w_up_ref, b_up_ref, w_dn_ref, b_dn_ref)
        scatter = jnp.where(hit, w_row, 0.0).astype(BF16)
        o_ref[...] += lax.dot_general(scatter, y, (((0,), (0,)), ((), ())),
                                      preferred_element_type=F32)
        return carry

    lax.fori_loop(0, n_sub, body, 0)


def _moe_call(used, over, h2, slot_t, w_t, opos_t, w_up, b_up, w_down, b_down):
    n_tok, d = h2.shape
    n_tiles = n_tok // MOE_TILE
    f2 = w_up.shape[2]
    f = w_down.shape[1]
    route_spec = pl.BlockSpec((N_EXPERTS, MOE_TILE), lambda i, e, *_: (0, i))
    grid_spec = pltpu.PrefetchScalarGridSpec(
        num_scalar_prefetch=2,
        grid=(n_tiles, N_EXPERTS),
        in_specs=[pl.BlockSpec((MOE_TILE, d), lambda i, e, *_: (i, 0),
                               pipeline_mode=pl.Buffered(1)),
                  route_spec, route_spec, route_spec,
                  pl.BlockSpec((1, d, f2), lambda i, e, *_: (e, 0, 0)),
                  pl.BlockSpec((1, 1, f2), lambda i, e, *_: (e, 0, 0)),
                  pl.BlockSpec((1, f, d), lambda i, e, *_: (e, 0, 0)),
                  pl.BlockSpec((1, 1, d), lambda i, e, *_: (e, 0, 0))],
        out_specs=pl.BlockSpec((MOE_TILE, d), lambda i, e, *_: (i, 0)),
        scratch_shapes=[pltpu.VMEM((N_GRANULES, MOE_GROUP * MOE_CAP, d), BF16),
                        pltpu.VMEM((N_GRANULES, MOE_GROUP * MOE_CAP, d), BF16)])
    return pl.pallas_call(
        _moe_kernel,
        grid_spec=grid_spec,
        out_shape=jax.ShapeDtypeStruct((n_tok, d), F32),
        compiler_params=pltpu.CompilerParams(
            dimension_semantics=("arbitrary", "arbitrary"),
            vmem_limit_bytes=V7X_VMEM_BYTES - 4 * 1024 * 1024),
    )(used, over, h2, slot_t, w_t, opos_t, w_up, b_up, w_down, b_down)


def _final_kernel(x_ref, y_ref, mod_ref, g_ref, b_ref, o_ref):
    gate2 = mod_ref[0, 5:6, :]
    o_ref[0] = _ln(DEEPNORM_ALPHA * x_ref[0] + gate2 * y_ref[0]) * g_ref[...] + b_ref[...]


def _final_call(x1, y, mod, g, b):
    bsz, seq, d = x1.shape
    tt = FINAL_TILE
    return pl.pallas_call(
        _final_kernel,
        grid=(bsz, seq // tt),
        in_specs=[pl.BlockSpec((1, tt, d), lambda i, j: (i, j, 0)),
                  pl.BlockSpec((1, tt, d), lambda i, j: (i, j, 0)),
                  pl.BlockSpec((1, N_MOD, d), lambda i, j: (i, 0, 0)),
                  pl.BlockSpec((1, d), lambda i, j: (0, 0)),
                  pl.BlockSpec((1, d), lambda i, j: (0, 0))],
        out_specs=pl.BlockSpec((1, tt, d), lambda i, j: (i, j, 0)),
        out_shape=jax.ShapeDtypeStruct((bsz, seq, d), F32),
    )(x1, y, mod, g, b)


def kernel(x, c, w_ada, b_ada, w_in, b_in, conv_w, conv_b, ln_a_g, ln_a_b, w_pa, b_pa, ln_v_g, ln_v_b, w_s, b_s, w_pb, b_pb, w_out, b_out, post1_g, post1_b, w_router, b_router, w_up, b_up, w_down, b_down, post2_g, post2_b):
    bsz, seq, d = x.shape
    assert w_ada.shape[0] == DEPTH == 1
    assert seq % MOE_TILE == 0 and MOE_TILE % MOE_GRANULE == 0 and MOE_GRANULE == SEQ_TILE
    assert SEQ_TILE % SUB_TILE == 0 and SUB_TILE % CHUNK == 0
    assert all(r % BF16_ROWS == 0 for r in MOE_FFN_ROWS) and MOE_FFN_ROWS[-1] == MOE_CAP
    assert list(MOE_FFN_ROWS) == sorted(set(MOE_FFN_ROWS))
    assert N_EXPERTS % MOE_GROUP == 0 and MOE_GROUP == V7X_SUBLANES
    l = 0
    row = lambda a: a.reshape(1, -1)
    mod = _ada_call(c, w_ada[l], b_ada[l]).reshape(bsz, N_MOD, d)
    bs_full = jnp.repeat(b_s[l].T, d // GMLP_HEADS, axis=1)
    x1 = _mixer_call(
        x, mod, w_in[l].astype(BF16), row(b_in[l]), conv_w[l], row(conv_b[l]),
        row(ln_a_g[l]), row(ln_a_b[l]), w_pa[l].astype(BF16), row(b_pa[l]),
        row(ln_v_g[l]), row(ln_v_b[l]), w_s[l], bs_full, w_pb[l].astype(BF16), row(b_pb[l]),
        w_out[l].astype(BF16), row(b_out[l]), row(post1_g[l]), row(post1_b[l]))
    h2, slot_t, w_t, opos_t, used, over = _route_call(x1, mod, w_router[l].T,
                                                      b_router[l].reshape(N_EXPERTS, 1))
    as_scalars = lambda a: a[:, :, 0].astype(jnp.int32).reshape(-1)
    y = _moe_call(as_scalars(used), as_scalars(over), h2.reshape(bsz * seq, d), slot_t, w_t, opos_t,
                  w_up[l].astype(BF16), b_up[l].reshape(N_EXPERTS, 1, -1),
                  w_down[l].astype(BF16), b_down[l].reshape(N_EXPERTS, 1, -1))
    return _final_call(x1, y.reshape(bsz, seq, d), mod, row(post2_g[l]), row(post2_b[l]))
```
